```python
import jax, jax.numpy as jnp
from jax import lax
import numpy as np

D_MODEL = 1024
BATCH = 4
SEQ = 8192
DEPTH = 2

DEEPNORM_ALPHA = (2 * DEPTH) ** 0.25
DEEPNORM_BETA = (8 * DEPTH) ** -0.25
LN_EPS = 1e-5

CONV_CH = D_MODEL
CONV_WIDTH = 31
SSM_INNER = D_MODEL
SSM_HEAD_DIM = 64
SSM_HEADS = SSM_INNER // SSM_HEAD_DIM
SSM_GROUPS = 2
SSM_HPG = SSM_HEADS // SSM_GROUPS
SSM_STATE = 128
SSM_CONV = 4
SSM_CHUNK = 128
SSM_XBC = SSM_INNER + 2 * SSM_GROUPS * SSM_STATE
EVEN_IN = 2 * CONV_CH + SSM_INNER + SSM_XBC + SSM_HEADS
EVEN_MIX = CONV_CH + SSM_INNER

GLA_HEADS = 4
GLA_DK_TOTAL = D_MODEL // 2
GLA_DV_TOTAL = D_MODEL
GLA_DK = GLA_DK_TOTAL // GLA_HEADS
GLA_DV = GLA_DV_TOTAL // GLA_HEADS
GLA_RANK = 16
GLA_TAU = 16.0
GLA_CHUNK = 64
ODD_IN = 2 * GLA_DK_TOTAL + 2 * GLA_DV_TOTAL + GLA_RANK

MEM_LEN = 256
XA_HEADS = 4
XA_HEAD_DIM = D_MODEL // XA_HEADS
D_FF = 4 * D_MODEL

N_EVEN = (DEPTH + 1) // 2
N_ODD = DEPTH // 2

kernel_name = "hybrid_conv_ssd_gla_deepnorm_trunk"


def layer_norm(x, g, b):
    xf = x.astype(jnp.float32)
    mu = jnp.mean(xf, -1, keepdims=True)
    var = jnp.mean(jnp.square(xf - mu), -1, keepdims=True)
    return ((xf - mu) * lax.rsqrt(var + LN_EPS) * g + b).astype(x.dtype)


def rms_norm(x, g):
    xf = x.astype(jnp.float32)
    return (xf * lax.rsqrt(jnp.mean(xf * xf, -1, keepdims=True) + LN_EPS) * g).astype(x.dtype)


def causal_depthwise_conv(x, w, b):
    width = w.shape[0]
    y = lax.conv_general_dilated(x, w[:, None, :].astype(x.dtype), window_strides=(1,),
                                 padding=[(width - 1, 0)],
                                 dimension_numbers=('NWC', 'WIO', 'NWC'),
                                 feature_group_count=x.shape[-1])
    return y + b


def ssd_chunked(x, a, b_mat, c_mat):
    bsz, seqlen = x.shape[:2]
    nc, q = seqlen // SSM_CHUNK, SSM_CHUNK
    xc = x.reshape(bsz, nc, q, SSM_GROUPS, SSM_HPG, SSM_HEAD_DIM)
    ac = jnp.moveaxis(a.astype(jnp.float32).reshape(bsz, nc, q, SSM_GROUPS, SSM_HPG), 2, -1)
    bm = b_mat.reshape(bsz, nc, q, SSM_GROUPS, SSM_STATE)
    cm = c_mat.reshape(bsz, nc, q, SSM_GROUPS, SSM_STATE)
    a_cs = jnp.cumsum(ac, axis=-1)
    causal = jnp.tril(jnp.ones((q, q), dtype=bool))
    seg = a_cs[..., :, None] - a_cs[..., None, :]
    decay = jnp.exp(jnp.where(causal, seg, -jnp.inf)).astype(x.dtype)
    cb = jnp.einsum('bclgn,bcsgn->bcgls', cm, bm)
    y_diag = jnp.einsum('bcgkls,bcsgkp->bclgkp', cb[:, :, :, None] * decay, xc)
    decay_to_end = jnp.exp(a_cs[..., -1:] - a_cs).astype(x.dtype)
    states = jnp.einsum('bcsgn,bcgks,bcsgkp->bcgkpn', bm, decay_to_end, xc)
    chunk_decay = jnp.exp(a_cs[..., -1])

    def step(h, inp):
        st, dec = inp
        return h * dec[..., None, None] + st, h

    h0 = jnp.zeros((bsz, SSM_GROUPS, SSM_HPG, SSM_HEAD_DIM, SSM_STATE), jnp.float32)
    _, prev = lax.scan(step, h0, (jnp.moveaxis(states, 1, 0), jnp.moveaxis(chunk_decay, 1, 0)))
    prev = jnp.moveaxis(prev, 0, 1)
    y_off = jnp.einsum('bclgn,bcgkpn,bcgkl->bclgkp', cm, prev, jnp.exp(a_cs))
    return (y_diag + y_off).astype(x.dtype).reshape(bsz, seqlen, SSM_HEADS, SSM_HEAD_DIM)


def even_mixer(x, w_in, conv_w, conv_b, conv_ln_g, conv_ln_b, ssm_conv_w, ssm_conv_b,
               dt_bias, a_log, d_skip, ssm_norm_g, w_out):
    bsz, seqlen, _ = x.shape
    proj = x @ w_in
    c0 = CONV_CH
    c1 = 2 * CONV_CH
    c2 = c1 + SSM_INNER
    c3 = c2 + SSM_XBC
    conv_val, conv_gate, z, xbc, dt = jnp.split(proj, [c0, c1, c2, c3], axis=-1)
    u = conv_val * jax.nn.sigmoid(conv_gate)
    u = causal_depthwise_conv(u, conv_w, conv_b)
    u = jax.nn.silu(layer_norm(u, conv_ln_g, conv_ln_b))
    xbc = jax.nn.silu(causal_depthwise_conv(xbc, ssm_conv_w, ssm_conv_b))
    xs, bm, cm = jnp.split(xbc, [SSM_INNER, SSM_INNER + SSM_GROUPS * SSM_STATE], axis=-1)
    dt = jax.nn.softplus((dt + dt_bias).astype(jnp.float32))
    a = -jnp.exp(a_log.astype(jnp.float32))
    xs_h = xs.reshape(bsz, seqlen, SSM_HEADS, SSM_HEAD_DIM)
    y = ssd_chunked((xs_h * dt[..., None]).astype(x.dtype), dt * a,
                    bm.reshape(bsz, seqlen, SSM_GROUPS, SSM_STATE),
                    cm.reshape(bsz, seqlen, SSM_GROUPS, SSM_STATE))
    y = (y + xs_h * d_skip[:, None]).reshape(bsz, seqlen, SSM_INNER) * jax.nn.silu(z)
    y = rms_norm(y.reshape(bsz, seqlen, SSM_GROUPS, SSM_INNER // SSM_GROUPS),
                 ssm_norm_g.reshape(SSM_GROUPS, SSM_INNER // SSM_GROUPS)).reshape(bsz, seqlen, SSM_INNER)
    return jnp.concatenate([u, y.astype(u.dtype)], axis=-1) @ w_out


def gla_chunked(q, k, v, log_g):
    bsz, seqlen = q.shape[:2]
    n, c = seqlen // GLA_CHUNK, GLA_CHUNK
    q = q.reshape(bsz, n, c, GLA_HEADS, GLA_DK)
    k = k.reshape(bsz, n, c, GLA_HEADS, GLA_DK)
    v = v.reshape(bsz, n, c, GLA_HEADS, GLA_DV)
    b_cs = jnp.cumsum(log_g.astype(jnp.float32).reshape(bsz, n, c, GLA_HEADS, GLA_DK), axis=2)
    q_t = q * jnp.exp(b_cs) * (GLA_DK ** -0.5)
    k_t = k * jnp.exp(-b_cs)
    k_end = k * jnp.exp(b_cs[:, :, -1:] - b_cs)
    causal = jnp.tril(jnp.ones((c, c), dtype=bool))
    att = jnp.where(causal, jnp.einsum('bnlhd,bnshd->bnhls', q_t, k_t), 0.0)
    o = jnp.einsum('bnhls,bnshv->bnlhv', att, v)
    kv = jnp.einsum('bnshd,bnshv->bnhdv', k_end, v)
    chunk_decay = jnp.exp(b_cs[:, :, -1])

    def step(s, inp):
        kv_c, dec = inp
        return s * dec[..., None] + kv_c, s

    s0 = jnp.zeros((bsz, GLA_HEADS, GLA_DK, GLA_DV), jnp.float32)
    _, prev = lax.scan(step, s0, (jnp.moveaxis(kv, 1, 0), jnp.moveaxis(chunk_decay, 1, 0)))
    prev = jnp.moveaxis(prev, 0, 1)
    o = o + jnp.einsum('bnlhd,bnhdv->bnlhv', q_t, prev)
    return o.astype(v.dtype).reshape(bsz, seqlen, GLA_HEADS, GLA_DV)


def odd_mixer(x, w_in, w_gate2, b_gate, head_norm_g, w_out):
    bsz, seqlen, _ = x.shape
    proj = x @ w_in
    s0 = GLA_DK_TOTAL
    s1 = 2 * GLA_DK_TOTAL
    s2 = s1 + GLA_DV_TOTAL
    s3 = s2 + GLA_DV_TOTAL
    q, k, v, g_out, g_low = jnp.split(proj, [s0, s1, s2, s3], axis=-1)
    log_g = jax.nn.log_sigmoid((g_low @ w_gate2 + b_gate).astype(jnp.float32)) / GLA_TAU
    o = gla_chunked(q.reshape(bsz, seqlen, GLA_HEADS, GLA_DK),
                    k.reshape(bsz, seqlen, GLA_HEADS, GLA_DK),
                    v.reshape(bsz, seqlen, GLA_HEADS, GLA_DV),
                    log_g.reshape(bsz, seqlen, GLA_HEADS, GLA_DK))
    o = rms_norm(o, head_norm_g).reshape(bsz, seqlen, GLA_DV_TOTAL) * jax.nn.silu(g_out)
    return o @ w_out


def memory_cross_attention(x, mem, w_q, w_k, w_v, w_o):
    bsz, seqlen, _ = x.shape
    q = (x @ w_q).reshape(bsz, seqlen, XA_HEADS, XA_HEAD_DIM)
    k = (mem @ w_k).reshape(bsz, mem.shape[1], XA_HEADS, XA_HEAD_DIM)
    v = (mem @ w_v).reshape(bsz, mem.shape[1], XA_HEADS, XA_HEAD_DIM)
    s = jnp.einsum('blhd,bmhd->bhlm', q, k).astype(jnp.float32) * (XA_HEAD_DIM ** -0.5)
    p = jax.nn.softmax(s, axis=-1).astype(v.dtype)
    o = jnp.einsum('bhlm,bmhd->blhd', p, v).reshape(bsz, seqlen, D_MODEL)
    return o @ w_o


def sq_relu_mlp(x, w1, w2):
    return jnp.square(jax.nn.relu(x @ w1)) @ w2


def setup_inputs(seed: int = 0) -> dict:
    key = jax.random.key(seed)
    ks = iter(jax.random.split(key, 40))
    f32 = jnp.float32

    def nrm(shape, scale):
        return jax.random.normal(next(ks), shape, f32) * scale

    dt0 = jnp.exp(jax.random.uniform(next(ks), (N_EVEN, SSM_HEADS), f32,
                                     np.log(1e-3).astype(np.float32), np.log(1e-1).astype(np.float32)))
    return {
        "x": nrm((BATCH, SEQ, D_MODEL), 1.0),
        "mem": nrm((BATCH, MEM_LEN, D_MODEL), 1.0),
        "even_w_in": nrm((N_EVEN, D_MODEL, EVEN_IN), D_MODEL ** -0.5),
        "even_conv_w": nrm((N_EVEN, CONV_WIDTH, CONV_CH), CONV_WIDTH ** -0.5),
        "even_conv_b": nrm((N_EVEN, CONV_CH), 0.01),
        "even_conv_ln_g": 1.0 + nrm((N_EVEN, CONV_CH), 0.02),
        "even_conv_ln_b": nrm((N_EVEN, CONV_CH), 0.02),
        "even_ssm_conv_w": nrm((N_EVEN, SSM_CONV, SSM_XBC), SSM_CONV ** -0.5),
        "even_ssm_conv_b": nrm((N_EVEN, SSM_XBC), 0.01),
        "even_dt_bias": dt0 + jnp.log(-jnp.expm1(-dt0)),
        "even_a_log": jnp.log(jax.random.uniform(next(ks), (N_EVEN, SSM_HEADS), f32, 1.0, 16.0)),
        "even_d_skip": 1.0 + nrm((N_EVEN, SSM_HEADS), 0.02),
        "even_ssm_norm_g": 1.0 + nrm((N_EVEN, SSM_INNER), 0.02),
        "even_w_out": nrm((N_EVEN, EVEN_MIX, D_MODEL), EVEN_MIX ** -0.5 * DEEPNORM_BETA),
        "odd_w_in": nrm((N_ODD, D_MODEL, ODD_IN), D_MODEL ** -0.5),
        "odd_w_gate2": nrm((N_ODD, GLA_RANK, GLA_DK_TOTAL), GLA_RANK ** -0.5),
        "odd_b_gate": nrm((N_ODD, GLA_DK_TOTAL), 0.1),
        "odd_head_norm_g": 1.0 + nrm((N_ODD, GLA_DV), 0.02),
        "odd_w_out": nrm((N_ODD, GLA_DV_TOTAL, D_MODEL), GLA_DV_TOTAL ** -0.5 * DEEPNORM_BETA),
        "xa_w_q": nrm((DEPTH, D_MODEL, D_MODEL), D_MODEL ** -0.5),
        "xa_w_k": nrm((DEPTH, D_MODEL, D_MODEL), D_MODEL ** -0.5),
        "xa_w_v": nrm((DEPTH, D_MODEL, D_MODEL), D_MODEL ** -0.5),
        "xa_w_o": nrm((DEPTH, D_MODEL, D_MODEL), D_MODEL ** -0.5 * DEEPNORM_BETA),
        "mlp_w1": nrm((DEPTH, D_MODEL, D_FF), D_MODEL ** -0.5),
        "mlp_w2": nrm((DEPTH, D_FF, D_MODEL), D_FF ** -0.5 * DEEPNORM_BETA),
        "ln_g": 1.0 + nrm((DEPTH, 3, D_MODEL), 0.02),
        "ln_b": nrm((DEPTH, 3, D_MODEL), 0.02),
    }


def reference(x, mem, even_w_in, even_conv_w, even_conv_b, even_conv_ln_g, even_conv_ln_b,
              even_ssm_conv_w, even_ssm_conv_b, even_dt_bias, even_a_log, even_d_skip,
              even_ssm_norm_g, even_w_out, odd_w_in, odd_w_gate2, odd_b_gate, odd_head_norm_g,
              odd_w_out, xa_w_q, xa_w_k, xa_w_v, xa_w_o, mlp_w1, mlp_w2, ln_g, ln_b):
    h = x
    for i in range(DEPTH):
        j = i // 2
        if i % 2 == 0:
            m = even_mixer(h, even_w_in[j], even_conv_w[j], even_conv_b[j], even_conv_ln_g[j],
                           even_conv_ln_b[j], even_ssm_conv_w[j], even_ssm_conv_b[j],
                           even_dt_bias[j], even_a_log[j], even_d_skip[j], even_ssm_norm_g[j],
                           even_w_out[j])
        else:
            m = odd_mixer(h, odd_w_in[j], odd_w_gate2[j], odd_b_gate[j], odd_head_norm_g[j],
                          odd_w_out[j])
        h = layer_norm(DEEPNORM_ALPHA * h + m, ln_g[i, 0], ln_b[i, 0])
        h = layer_norm(DEEPNORM_ALPHA * h + memory_cross_attention(h, mem, xa_w_q[i], xa_w_k[i],
                                                                   xa_w_v[i], xa_w_o[i]),
                       ln_g[i, 1], ln_b[i, 1])
        h = layer_norm(DEEPNORM_ALPHA * h + sq_relu_mlp(h, mlp_w1[i], mlp_w2[i]),
                       ln_g[i, 2], ln_b[i, 2])
    return h
```

```python
import functools

import jax
import jax.numpy as jnp
from jax import lax
from jax.experimental import pallas as pl
from jax.experimental.pallas import tpu as pltpu

F32 = jnp.float32
BF16 = jnp.bfloat16

LN_EPS = 1e-5
LANES = 128
VMEM_LIMIT_BYTES = 56 * 1024 * 1024

SSM_HEAD_DIM = 64
SSM_GROUPS = 2
SSM_STATE = 128
SSM_CHUNK = 128
GLA_HEADS = 4
GLA_RANK = 16
GLA_TAU = 16.0
GLA_CHUNK = 64
XA_HEADS = 4
CONV_HALO = 32
SSM_CONV_HALO = 8

ROW_TILE = 512
MLP_CHUNKS = 4


def _dot(a, b):
    return jnp.dot(a, b, preferred_element_type=F32)


def _dot_nt(a, b):
    return lax.dot_general(a, b, (((1,), (1,)), ((), ())), preferred_element_type=F32)


def _dot_tn(a, b):
    return lax.dot_general(a, b, (((0,), (0,)), ((), ())), preferred_element_type=F32)


def _layer_norm(x, g, b):
    mu = jnp.mean(x, -1, keepdims=True)
    xc = x - mu
    var = jnp.mean(xc * xc, -1, keepdims=True)
    return xc * lax.rsqrt(var + LN_EPS) * g + b


def _sigmoid(x):
    return 1.0 / (1.0 + jnp.exp(-x))


def _silu(x):
    return x * _sigmoid(x)


def _softplus(x):
    return jnp.maximum(x, 0.0) + jnp.log1p(jnp.exp(-jnp.abs(x)))


def _tri_incl(n):
    r = lax.broadcasted_iota(jnp.int32, (n, n), 0)
    c = lax.broadcasted_iota(jnp.int32, (n, n), 1)
    return (c <= r).astype(F32)


def _cumsum_rows(tri, x):
    return jnp.dot(tri, x, preferred_element_type=F32, precision=lax.Precision.HIGHEST)


def _expand_lanes(x, e):
    hi = x.astype(BF16)
    lo = (x - hi.astype(F32)).astype(BF16)
    return _dot(hi, e) + _dot(lo, e)


def _const_spec(shape):
    zeros = (0,) * len(shape)
    return pl.BlockSpec(shape, lambda *_: zeros, pipeline_mode=pl.Buffered(1))


def _params(n_grid):
    return pltpu.CompilerParams(dimension_semantics=("arbitrary",) * n_grid,
                                vmem_limit_bytes=VMEM_LIMIT_BYTES)


def _mlp_kernel(h_ref, w1_ref, w2_ref, g_ref, b_ref, o_ref, *, alpha):
    h = h_ref[...]
    hb = h.astype(BF16)
    cw = w1_ref.shape[1] // MLP_CHUNKS
    acc = None
    for j in range(MLP_CHUNKS):
        a = jnp.maximum(_dot(hb, w1_ref[:, j * cw:(j + 1) * cw]), 0.0)
        p = _dot((a * a).astype(BF16), w2_ref[j * cw:(j + 1) * cw, :])
        acc = p if acc is None else acc + p
    o_ref[...] = _layer_norm(alpha * h + acc, g_ref[...], b_ref[...])


def _mlp(h, w1, w2, g, b, alpha):
    t, d = h.shape
    dff = w1.shape[1]
    row = pl.BlockSpec((ROW_TILE, d), lambda i: (i, 0))
    return pl.pallas_call(
        functools.partial(_mlp_kernel, alpha=alpha),
        grid=(t // ROW_TILE,),
        in_specs=[row, _const_spec((d, dff)), _const_spec((dff, d)), _const_spec((1, d)), _const_spec((1, d))],
        out_specs=row,
        out_shape=jax.ShapeDtypeStruct((t, d), F32),
        compiler_params=_params(1),
        name="mlp",
    )(h, w1, w2, g, b)


def _kv_kernel(mem_ref, wk_ref, wv_ref, kt_ref, v_ref):
    mb = mem_ref[0].astype(BF16)
    kt_ref[0, 0] = _dot(mb, wk_ref[0]).T.astype(BF16)
    v_ref[0, 0] = _dot(mb, wv_ref[0]).astype(BF16)


def _kv(mem, wk, wv):
    bsz, m, d = mem.shape
    depth = wk.shape[0]
    w_spec = pl.BlockSpec((1, d, d), lambda i, b: (i, 0, 0))
    return pl.pallas_call(
        _kv_kernel,
        grid=(depth, bsz),
        in_specs=[pl.BlockSpec((1, m, d), lambda i, b: (b, 0, 0)), w_spec, w_spec],
        out_specs=[pl.BlockSpec((1, 1, d, m), lambda i, b: (i, b, 0, 0)),
                   pl.BlockSpec((1, 1, m, d), lambda i, b: (i, b, 0, 0))],
        out_shape=[jax.ShapeDtypeStruct((depth, bsz, d, m), BF16),
                   jax.ShapeDtypeStruct((depth, bsz, m, d), BF16)],
        compiler_params=_params(2),
        name="kv_proj",
    )(mem, wk, wv)


def _xattn_kernel(h_ref, kt_ref, v_ref, wq_ref, wo_ref, g_ref, b_ref, o_ref, *, alpha):
    h = h_ref[0]
    d = h.shape[-1]
    dh = d // XA_HEADS
    q = (_dot(h.astype(BF16), wq_ref[...]) * (dh ** -0.5)).astype(BF16)
    outs = []
    for hd in range(XA_HEADS):
        s = _dot(q[:, hd * dh:(hd + 1) * dh], kt_ref[0, hd * dh:(hd + 1) * dh, :])
        e = jnp.exp(s - jnp.max(s, -1, keepdims=True))
        p = e / jnp.sum(e, -1, keepdims=True)
        outs.append(_dot(p.astype(BF16), v_ref[0, :, hd * dh:(hd + 1) * dh]).astype(BF16))
    m = _dot(jnp.concatenate(outs, -1), wo_ref[...])
    o_ref[0] = _layer_norm(alpha * h + m, g_ref[...], b_ref[...])


def _xattn(h, kt, v, wq, wo, g, b, alpha):
    bsz, l, d = h.shape
    m = v.shape[1]
    row = pl.BlockSpec((1, ROW_TILE, d), lambda bi, li: (bi, li, 0))
    return pl.pallas_call(
        functools.partial(_xattn_kernel, alpha=alpha),
        grid=(bsz, l // ROW_TILE),
        in_specs=[row,
                  pl.BlockSpec((1, d, m), lambda bi, li: (bi, 0, 0)),
                  pl.BlockSpec((1, m, d), lambda bi, li: (bi, 0, 0)),
                  _const_spec((d, d)), _const_spec((d, d)), _const_spec((1, d)), _const_spec((1, d))],
        out_specs=row,
        out_shape=jax.ShapeDtypeStruct((bsz, l, d), F32),
        compiler_params=_params(2),
        name="xattn",
    )(h, kt, v, wq, wo, g, b)


def _gla_kernel(h_ref, wq_ref, wk_ref, wv_ref, wg_ref, wlow_ref, wgate2_ref, bgate_ref, hng_ref,
                wout_ref, g_ref, b_ref, o_ref,
                st_ref, q_s, k_s, v_s, bcs_s, o_s, *, alpha):
    @pl.when(pl.program_id(1) == 0)
    def _():
        st_ref[...] = jnp.zeros_like(st_ref)

    h = h_ref[0]
    hb = h.astype(BF16)
    tl = h.shape[0]
    dk = q_s.shape[1] // GLA_HEADS
    dv = v_s.shape[1] // GLA_HEADS
    cs = GLA_CHUNK

    q_s[...] = _dot(hb, wq_ref[...])
    k_s[...] = _dot(hb, wk_ref[...])
    v_s[...] = _dot(hb, wv_ref[...])
    g_low = _dot(hb, wlow_ref[...])
    gate = _dot(g_low.astype(BF16), wgate2_ref[...]) + bgate_ref[...]
    bcs_s[...] = -_softplus(-gate) * (1.0 / GLA_TAU)

    tri = _tri_incl(cs)
    causal = tri > 0.0

    def chunk(c, carry):
        r0 = pl.multiple_of(c * cs, cs)
        rows = pl.ds(r0, cs)
        b_all = _cumsum_rows(tri, bcs_s[rows, :])
        for hd in range(GLA_HEADS):
            kcols = slice(hd * dk, (hd + 1) * dk)
            vcols = slice(hd * dv, (hd + 1) * dv)
            bc = b_all[:, kcols]
            b_last = bc[cs - 1:cs, :]
            qh = q_s[rows, kcols]
            kh = k_s[rows, kcols]
            vh = v_s[rows, vcols].astype(BF16)
            q_t = (qh * jnp.exp(bc) * (dk ** -0.5)).astype(BF16)
            k_t = (kh * jnp.exp(-bc)).astype(BF16)
            k_end = (kh * jnp.exp(b_last - bc)).astype(BF16)
            att = jnp.where(causal, _dot_nt(q_t, k_t), 0.0).astype(BF16)
            st = st_ref[hd]
            o_s[rows, vcols] = _dot(att, vh) + _dot_nt(q_t, st.astype(BF16))
            st_ref[hd] = st * jnp.exp(b_last) + _dot_tn(vh, k_end)
        return carry

    lax.fori_loop(0, tl // cs, chunk, 0)

    gate_out = _silu(_dot(hb, wg_ref[...]))
    outs = []
    for hd in range(GLA_HEADS):
        vcols = slice(hd * dv, (hd + 1) * dv)
        oh = o_s[:, vcols]
        oh = oh * lax.rsqrt(jnp.mean(oh * oh, -1, keepdims=True) + LN_EPS) * hng_ref[...]
        outs.append((oh * gate_out[:, vcols]).astype(BF16))
    m = _dot(jnp.concatenate(outs, -1), wout_ref[...])
    o_ref[0] = _layer_norm(alpha * h + m, g_ref[...], b_ref[...])


def _gla_mixer(h, w_in, w_gate2, b_gate, head_norm_g, w_out, g, b, alpha):
    bsz, l, d = h.shape
    dkt = w_gate2.shape[1]
    dvt = w_out.shape[0]
    s1, s2, s3 = dkt, 2 * dkt, 2 * dkt + dvt
    s4 = s3 + dvt
    wb = w_in.astype(BF16)
    wq, wk, wv, wg = wb[:, :s1], wb[:, s1:s2], wb[:, s2:s3], wb[:, s3:s4]
    wlow = jnp.pad(wb[:, s4:], ((0, 0), (0, LANES - GLA_RANK)))
    wgate2 = jnp.pad(w_gate2.astype(BF16), ((0, LANES - GLA_RANK), (0, 0)))
    dk = dkt // GLA_HEADS
    dv = dvt // GLA_HEADS
    tl = ROW_TILE
    row = pl.BlockSpec((1, tl, d), lambda bi, li: (bi, li, 0))
    return pl.pallas_call(
        functools.partial(_gla_kernel, alpha=alpha),
        grid=(bsz, l // tl),
        in_specs=[row, _const_spec((d, dkt)), _const_spec((d, dkt)), _const_spec((d, dvt)), _const_spec((d, dvt)),
                  _const_spec((d, LANES)), _const_spec((LANES, dkt)), _const_spec((1, dkt)), _const_spec((1, dv)),
                  _const_spec((dvt, d)), _const_spec((1, d)), _const_spec((1, d))],
        out_specs=row,
        out_shape=jax.ShapeDtypeStruct((bsz, l, d), F32),
        scratch_shapes=[pltpu.VMEM((GLA_HEADS, dv, dk), F32),
                        pltpu.VMEM((tl, dkt), F32), pltpu.VMEM((tl, dkt), F32), pltpu.VMEM((tl, dvt), F32),
                        pltpu.VMEM((tl, dkt), F32), pltpu.VMEM((tl, dvt), F32)],
        compiler_params=_params(2),
        name="gla_mixer",
    )(h, wq, wk, wv, wg, wlow, wgate2, b_gate.reshape(1, dkt), head_norm_g.reshape(1, dv),
      w_out.astype(BF16), g, b)


def _causal_conv(buf_ref, w_ref, halo, n_rows, row_block):
    width = w_ref.shape[0]
    base = halo - (width - 1)
    blocks = []
    for r0 in range(0, n_rows, row_block):
        acc = None
        for k in range(width):
            term = w_ref[k:k + 1, :] * buf_ref[r0 + base + k:r0 + base + k + row_block, :]
            acc = term if acc is None else acc + term
        blocks.append(acc)
    return blocks


def _even_kernel(h_ref, wval_ref, wgate_ref, wz_ref, wxbc_ref, wdt_ref,
                 cw_ref, cb_ref, clg_ref, clb_ref, sw_ref, sb_ref, dtb_ref, alog_ref, dskip_ref, sng_ref,
                 wou_ref, woy_ref, g_ref, b_ref, o_ref,
                 ubuf, xbuf, xs_s, b_s, c_s, acol_s, y_s, u_s, st_ref, *, alpha):
    tl = h_ref.shape[1]
    d_inner = xs_s.shape[1]
    gn = SSM_GROUPS * SSM_STATE
    hpg_cols = d_inner // SSM_GROUPS
    n_heads = d_inner // SSM_HEAD_DIM
    q = SSM_CHUNK

    @pl.when(pl.program_id(1) == 0)
    def _():
        ubuf[0:CONV_HALO, :] = jnp.zeros((CONV_HALO, ubuf.shape[1]), F32)
        xbuf[0:SSM_CONV_HALO, :] = jnp.zeros((SSM_CONV_HALO, xbuf.shape[1]), F32)
        st_ref[...] = jnp.zeros_like(st_ref)

    h = h_ref[0]
    hb = h.astype(BF16)

    ubuf[CONV_HALO:CONV_HALO + tl, :] = _dot(hb, wval_ref[...]) * _sigmoid(_dot(hb, wgate_ref[...]))
    rb = 32
    for i, blk in enumerate(_causal_conv(ubuf, cw_ref, CONV_HALO, tl, rb)):
        u = _layer_norm(blk + cb_ref[...], clg_ref[...], clb_ref[...])
        u_s[i * rb:(i + 1) * rb, :] = _silu(u).astype(BF16)
    ubuf[0:CONV_HALO, :] = ubuf[tl:tl + CONV_HALO, :]

    xbuf[SSM_CONV_HALO:SSM_CONV_HALO + tl, :] = _dot(hb, wxbc_ref[...])
    rb = 64
    for i, blk in enumerate(_causal_conv(xbuf, sw_ref, SSM_CONV_HALO, tl, rb)):
        xbc = _silu(blk + sb_ref[...])
        xs_s[i * rb:(i + 1) * rb, :] = xbc[:, :d_inner]
        b_s[i * rb:(i + 1) * rb, :] = xbc[:, d_inner:d_inner + gn]
        c_s[i * rb:(i + 1) * rb, :] = xbc[:, d_inner + gn:]
    xbuf[0:SSM_CONV_HALO, :] = xbuf[tl:tl + SSM_CONV_HALO, :]

    dt = _softplus(_dot(hb, wdt_ref[...]) + dtb_ref[...])
    a_neg = -jnp.exp(alog_ref[...])
    acol_s[...] = dt * a_neg
    hh = lax.broadcasted_iota(jnp.int32, (LANES, d_inner), 0)
    cc = lax.broadcasted_iota(jnp.int32, (LANES, d_inner), 1)
    expand = (cc // SSM_HEAD_DIM == hh).astype(BF16)
    y_s[...] = xs_s[...] * _expand_lanes(dt, expand)

    tri = _tri_incl(q)
    causal = tri > 0.0
    lane = lax.broadcasted_iota(jnp.int32, (q, LANES), 1)
    left = lane < SSM_HEAD_DIM

    def chunk(c, carry):
        r0 = pl.multiple_of(c * q, q)
        rows = pl.ds(r0, q)
        acs = _cumsum_rows(tri, acol_s[rows, :])
        acs_t = acs.T
        a_last = acs[q - 1:q, :]
        factors = jnp.concatenate([jnp.exp(a_last - acs), jnp.exp(acs),
                                   jnp.broadcast_to(jnp.exp(a_last), (8, LANES))], 0)
        factors = _expand_lanes(factors, expand)
        dend_full = factors[0:q]
        ein_full = factors[q:2 * q]
        cdec_full = factors[2 * q:2 * q + 1]
        xdt = y_s[rows, :]
        xdt_b = xdt.astype(BF16)
        xend_b = (xdt * dend_full).astype(BF16)
        for gi in range(SSM_GROUPS):
            gcols = slice(gi * hpg_cols, (gi + 1) * hpg_cols)
            bg = b_s[rows, gi * SSM_STATE:(gi + 1) * SSM_STATE]
            cg = c_s[rows, gi * SSM_STATE:(gi + 1) * SSM_STATE].astype(BF16)
            bg_b = bg.astype(BF16)
            cb = _dot_nt(cg, bg_b)
            st = st_ref[gi]
            y_off = _dot(cg, st.astype(BF16)) * ein_full[:, gcols]
            st_ref[gi] = st * cdec_full[:, gcols] + _dot(bg.T.astype(BF16), xend_b[:, gcols])
            for pair in range(hpg_cols // LANES):
                col0 = gi * hpg_cols + pair * LANES
                xp = xdt_b[:, col0:col0 + LANES]
                yp = None
                for half in range(LANES // SSM_HEAD_DIM):
                    hd = col0 // SSM_HEAD_DIM + half
                    seg = acs[:, hd:hd + 1] - acs_t[hd:hd + 1, :]
                    m_h = (cb * jnp.exp(jnp.where(causal, seg, -jnp.inf))).astype(BF16)
                    keep = left if half == 0 else jnp.logical_not(left)
                    part = _dot(m_h, jnp.where(keep, xp, jnp.zeros_like(xp)))
                    yp = part if yp is None else yp + part
                y_s[rows, col0:col0 + LANES] = yp + y_off[:, pair * LANES:(pair + 1) * LANES]
        return carry

    lax.fori_loop(0, tl // q, chunk, 0)

    z = _dot(hb, wz_ref[...])
    y = (y_s[...] + xs_s[...] * dskip_ref[...]) * _silu(z)
    outs = []
    for gi in range(SSM_GROUPS):
        gcols = slice(gi * hpg_cols, (gi + 1) * hpg_cols)
        yg = y[:, gcols]
        yg = yg * lax.rsqrt(jnp.mean(yg * yg, -1, keepdims=True) + LN_EPS) * sng_ref[:, gcols]
        outs.append(yg.astype(BF16))
    m = _dot(u_s[...], wou_ref[...]) + _dot(jnp.concatenate(outs, -1), woy_ref[...])
    o_ref[0] = _layer_norm(alpha * h + m, g_ref[...], b_ref[...])


def _even_mixer(h, w_in, conv_w, conv_b, conv_ln_g, conv_ln_b, ssm_conv_w, ssm_conv_b, dt_bias, a_log,
                d_skip, ssm_norm_g, w_out, g, b, alpha):
    bsz, l, d = h.shape
    conv_ch = conv_w.shape[1]
    xbc_w = ssm_conv_w.shape[1]
    n_heads = a_log.shape[0]
    d_inner = n_heads * SSM_HEAD_DIM
    gn = SSM_GROUPS * SSM_STATE
    assert xbc_w == d_inner + 2 * gn
    c0, c1 = conv_ch, 2 * conv_ch
    c2 = c1 + d_inner
    c3 = c2 + xbc_w
    wb = w_in.astype(BF16)
    wval, wgate, wz, wxbc = wb[:, :c0], wb[:, c0:c1], wb[:, c1:c2], wb[:, c2:c3]
    pad_h = LANES - n_heads
    wdt = jnp.pad(wb[:, c3:], ((0, 0), (0, pad_h)))
    dtb = jnp.pad(dt_bias, (0, pad_h)).reshape(1, LANES)
    alog = jnp.pad(a_log, (0, pad_h)).reshape(1, LANES)
    dskip = jnp.repeat(d_skip, SSM_HEAD_DIM).reshape(1, d_inner)
    wo = w_out.astype(BF16)
    tl = ROW_TILE
    row = pl.BlockSpec((1, tl, d), lambda bi, li: (bi, li, 0))
    cs = _const_spec
    return pl.pallas_call(
        functools.partial(_even_kernel, alpha=alpha),
        grid=(bsz, l // tl),
        in_specs=[row, cs((d, conv_ch)), cs((d, conv_ch)), cs((d, d_inner)), cs((d, xbc_w)), cs((d, LANES)),
                  cs(conv_w.shape), cs((1, conv_ch)), cs((1, conv_ch)), cs((1, conv_ch)),
                  cs(ssm_conv_w.shape), cs((1, xbc_w)), cs((1, LANES)), cs((1, LANES)), cs((1, d_inner)),
                  cs((1, d_inner)), cs((conv_ch, d)), cs((d_inner, d)), cs((1, d)), cs((1, d))],
        out_specs=row,
        out_shape=jax.ShapeDtypeStruct((bsz, l, d), F32),
        scratch_shapes=[pltpu.VMEM((tl + CONV_HALO, conv_ch), F32),
                        pltpu.VMEM((tl + SSM_CONV_HALO, xbc_w), F32),
                        pltpu.VMEM((tl, d_inner), F32), pltpu.VMEM((tl, gn), F32), pltpu.VMEM((tl, gn), F32),
                        pltpu.VMEM((tl, LANES), F32), pltpu.VMEM((tl, d_inner), F32),
                        pltpu.VMEM((tl, conv_ch), BF16),
                        pltpu.VMEM((SSM_GROUPS, SSM_STATE, d_inner // SSM_GROUPS), F32)],
        compiler_params=_params(2),
        name="conv_ssd_mixer",
    )(h, wval, wgate, wz, wxbc, wdt, conv_w, conv_b.reshape(1, -1), conv_ln_g.reshape(1, -1),
      conv_ln_b.reshape(1, -1), ssm_conv_w, ssm_conv_b.reshape(1, -1), dtb, alog, dskip,
      ssm_norm_g.reshape(1, -1), wo[:conv_ch], wo[conv_ch:], g, b)


def kernel(x, mem, even_w_in, even_conv_w, even_conv_b, even_conv_ln_g, even_conv_ln_b, even_ssm_conv_w,
           even_ssm_conv_b, even_dt_bias, even_a_log, even_d_skip, even_ssm_norm_g, even_w_out, odd_w_in,
           odd_w_gate2, odd_b_gate, odd_head_norm_g, odd_w_out, xa_w_q, xa_w_k, xa_w_v, xa_w_o, mlp_w1,
           mlp_w2, ln_g, ln_b):
    bsz, l, d = x.shape
    depth = ln_g.shape[0]
    alpha = float((2 * depth) ** 0.25)
    kt_all, v_all = _kv(mem, xa_w_k.astype(BF16), xa_w_v.astype(BF16))
    h = x
    for i in range(depth):
        j = i // 2
        g = [ln_g[i, s].reshape(1, d) for s in range(3)]
        b = [ln_b[i, s].reshape(1, d) for s in range(3)]
        if i % 2 == 0:
            h = _even_mixer(h, even_w_in[j], even_conv_w[j], even_conv_b[j], even_conv_ln_g[j],
                            even_conv_ln_b[j], even_ssm_conv_w[j], even_ssm_conv_b[j], even_dt_bias[j],
                            even_a_log[j], even_d_skip[j], even_ssm_norm_g[j], even_w_out[j], g[0], b[0], alpha)
        else:
            h = _gla_mixer(h, odd_w_in[j], odd_w_gate2[j], odd_b_gate[j], odd_head_norm_g[j], odd_w_out[j],
                           g[0], b[0], alpha)
        h = _xattn(h, kt_all[i], v_all[i], xa_w_q[i].astype(BF16), xa_w_o[i].astype(BF16), g[1], b[1], alpha)
        h = _mlp(h.reshape(bsz * l, d), mlp_w1[i].astype(BF16), mlp_w2[i].astype(BF16), g[2], b[2],
                 alpha).reshape(bsz, l, d)
    return h
```

```python
import functools

import jax
import jax.numpy as jnp
from jax import lax
from jax.experimental import pallas as pl
from jax.experimental.pallas import tpu as pltpu

F32 = jnp.float32
BF16 = jnp.bfloat16

LN_EPS = 1e-5
LANES = 128
VMEM_LIMIT_BYTES = 56 * 1024 * 1024

SSM_HEAD_DIM = 64
SSM_GROUPS = 2
SSM_STATE = 128
SSM_CHUNK = 128
GLA_HEADS = 4
GLA_RANK = 16
GLA_TAU = 16.0
GLA_CHUNK = 64
XA_HEADS = 4
CONV_HALO = 32
SSM_CONV_HALO = 8
SUBLANES = 8
CONV_ROWS = 128

ROW_TILE = 512
MLP_CHUNKS = 4


def _dot(a, b):
    return jnp.dot(a, b, preferred_element_type=F32)


def _dot_nt(a, b):
    return lax.dot_general(a, b, (((1,), (1,)), ((), ())), preferred_element_type=F32)


def _dot_tn(a, b):
    return lax.dot_general(a, b, (((0,), (0,)), ((), ())), preferred_element_type=F32)


def _layer_norm(x, g, b):
    mu = jnp.mean(x, -1, keepdims=True)
    xc = x - mu
    var = jnp.mean(xc * xc, -1, keepdims=True)
    return xc * lax.rsqrt(var + LN_EPS) * g + b


def _sigmoid(x):
    return 0.5 * jnp.tanh(0.5 * x) + 0.5


def _silu(x):
    hx = 0.5 * x
    return hx * jnp.tanh(hx) + hx


def _softplus(x):
    return jnp.maximum(x, 0.0) + jnp.log1p(jnp.exp(-jnp.abs(x)))


def _tri_incl(n):
    r = lax.broadcasted_iota(jnp.int32, (n, n), 0)
    c = lax.broadcasted_iota(jnp.int32, (n, n), 1)
    return (c <= r).astype(F32)


def _cumsum_rows(tri, x):
    return jnp.dot(tri, x, preferred_element_type=F32, precision=lax.Precision.HIGHEST)


def _expand_lanes(x, e):
    hi = x.astype(BF16)
    lo = (x - hi.astype(F32)).astype(BF16)
    return _dot(hi, e) + _dot(lo, e)


def _const_spec(shape):
    zeros = (0,) * len(shape)
    return pl.BlockSpec(shape, lambda *_: zeros, pipeline_mode=pl.Buffered(1))


def _params(n_grid):
    return pltpu.CompilerParams(dimension_semantics=("arbitrary",) * n_grid,
                                vmem_limit_bytes=VMEM_LIMIT_BYTES)


def _mlp_kernel(h_ref, w1_ref, w2_ref, g_ref, b_ref, o_ref, *, alpha):
    h = h_ref[...]
    hb = h.astype(BF16)
    cw = w1_ref.shape[1] // MLP_CHUNKS
    acc = None
    for j in range(MLP_CHUNKS):
        a = jnp.maximum(_dot(hb, w1_ref[:, j * cw:(j + 1) * cw]), 0.0)
        p = _dot((a * a).astype(BF16), w2_ref[j * cw:(j + 1) * cw, :])
        acc = p if acc is None else acc + p
    o_ref[...] = _layer_norm(alpha * h + acc, g_ref[...], b_ref[...])


def _mlp(h, w1, w2, g, b, alpha):
    t, d = h.shape
    dff = w1.shape[1]
    row = pl.BlockSpec((ROW_TILE, d), lambda i: (i, 0))
    return pl.pallas_call(
        functools.partial(_mlp_kernel, alpha=alpha),
        grid=(t // ROW_TILE,),
        in_specs=[row, _const_spec((d, dff)), _const_spec((dff, d)), _const_spec((1, d)), _const_spec((1, d))],
        out_specs=row,
        out_shape=jax.ShapeDtypeStruct((t, d), F32),
        compiler_params=_params(1),
        name="mlp",
    )(h, w1, w2, g, b)


def _kv_kernel(mem_ref, wk_ref, wv_ref, kt_ref, v_ref):
    mb = mem_ref[0].astype(BF16)
    kt_ref[0, 0] = _dot(mb, wk_ref[0]).T.astype(BF16)
    v_ref[0, 0] = _dot(mb, wv_ref[0]).astype(BF16)


def _kv(mem, wk, wv):
    bsz, m, d = mem.shape
    depth = wk.shape[0]
    w_spec = pl.BlockSpec((1, d, d), lambda i, b: (i, 0, 0))
    return pl.pallas_call(
        _kv_kernel,
        grid=(depth, bsz),
        in_specs=[pl.BlockSpec((1, m, d), lambda i, b: (b, 0, 0)), w_spec, w_spec],
        out_specs=[pl.BlockSpec((1, 1, d, m), lambda i, b: (i, b, 0, 0)),
                   pl.BlockSpec((1, 1, m, d), lambda i, b: (i, b, 0, 0))],
        out_shape=[jax.ShapeDtypeStruct((depth, bsz, d, m), BF16),
                   jax.ShapeDtypeStruct((depth, bsz, m, d), BF16)],
        compiler_params=_params(2),
        name="kv_proj",
    )(mem, wk, wv)


def _xattn_kernel(h_ref, kt_ref, v_ref, wq_ref, wo_ref, g_ref, b_ref, o_ref, *, alpha):
    h = h_ref[0]
    d = h.shape[-1]
    dh = d // XA_HEADS
    q = (_dot(h.astype(BF16), wq_ref[...]) * (dh ** -0.5)).astype(BF16)
    outs = []
    for hd in range(XA_HEADS):
        s = _dot(q[:, hd * dh:(hd + 1) * dh], kt_ref[0, hd * dh:(hd + 1) * dh, :])
        e = jnp.exp(s - jnp.max(s, -1, keepdims=True))
        p = e / jnp.sum(e, -1, keepdims=True)
        outs.append(_dot(p.astype(BF16), v_ref[0, :, hd * dh:(hd + 1) * dh]).astype(BF16))
    m = _dot(jnp.concatenate(outs, -1), wo_ref[...])
    o_ref[0] = _layer_norm(alpha * h + m, g_ref[...], b_ref[...])


def _xattn(h, kt, v, wq, wo, g, b, alpha):
    bsz, l, d = h.shape
    m = v.shape[1]
    row = pl.BlockSpec((1, ROW_TILE, d), lambda bi, li: (bi, li, 0))
    return pl.pallas_call(
        functools.partial(_xattn_kernel, alpha=alpha),
        grid=(bsz, l // ROW_TILE),
        in_specs=[row,
                  pl.BlockSpec((1, d, m), lambda bi, li: (bi, 0, 0)),
                  pl.BlockSpec((1, m, d), lambda bi, li: (bi, 0, 0)),
                  _const_spec((d, d)), _const_spec((d, d)), _const_spec((1, d)), _const_spec((1, d))],
        out_specs=row,
        out_shape=jax.ShapeDtypeStruct((bsz, l, d), F32),
        compiler_params=_params(2),
        name="xattn",
    )(h, kt, v, wq, wo, g, b)


def _gla_kernel(h_ref, wq_ref, wk_ref, wv_ref, wg_ref, wlow_ref, wgate2_ref, bgate_ref, hng_ref,
                wout_ref, g_ref, b_ref, o_ref,
                st_ref, q_s, k_s, v_s, bcs_s, o_s, qt_s, kv_s, sb_s, *, alpha):
    @pl.when(pl.program_id(1) == 0)
    def _():
        st_ref[...] = jnp.zeros_like(st_ref)

    h = h_ref[0]
    hb = h.astype(BF16)
    tl = h.shape[0]
    dk = q_s.shape[1] // GLA_HEADS
    dv = v_s.shape[1] // GLA_HEADS
    cs = GLA_CHUNK

    q_s[...] = _dot(hb, wq_ref[...])
    k_s[...] = _dot(hb, wk_ref[...])
    v_s[...] = _dot(hb, wv_ref[...]).astype(BF16)
    g_low = _dot(hb, wlow_ref[...])
    gate = _dot(g_low.astype(BF16), wgate2_ref[...]) + bgate_ref[...]
    bcs_s[...] = -_softplus(-gate) * (1.0 / GLA_TAU)

    tri = _tri_incl(cs)
    causal = tri > 0.0

    n_chunks = tl // cs
    heads = [(slice(hd * dk, (hd + 1) * dk), slice(hd * dv, (hd + 1) * dv)) for hd in range(GLA_HEADS)]

    chunk_rows = [slice(c * cs, (c + 1) * cs) for c in range(n_chunks)]
    dkt = bcs_s.shape[1]
    b_wide = _cumsum_rows(tri, jnp.concatenate([bcs_s[r, :] for r in chunk_rows], 1))
    q_t, k_t, k_end, chunk_decay = [], [], [], []
    for c, r in enumerate(chunk_rows):
        b_all = b_wide[:, c * dkt:(c + 1) * dkt]
        b_last = b_all[cs - 1:cs, :]
        kc = k_s[r, :]
        q_t.append((q_s[r, :] * jnp.exp(b_all) * (dk ** -0.5)).astype(BF16))
        k_t.append((kc * jnp.exp(-b_all)).astype(BF16))
        k_end.append((kc * jnp.exp(b_last - b_all)).astype(BF16))
        chunk_decay.append(jnp.exp(b_last))
        qt_s[r, :] = q_t[c]
    att = [[jnp.where(causal, _dot_nt(q_t[c][:, kcols], k_t[c][:, kcols]), 0.0).astype(BF16)
            for kcols, _ in heads] for c in range(n_chunks)]
    for c, r in enumerate(chunk_rows):
        for hd, (kcols, vcols) in enumerate(heads):
            o_s[r, vcols] = _dot(att[c][hd], v_s[r, vcols])
    for c, r in enumerate(chunk_rows):
        for hd, (kcols, vcols) in enumerate(heads):
            kv_s[c, hd] = _dot_tn(v_s[r, vcols], k_end[c][:, kcols])

    for hd, (kcols, _) in enumerate(heads):
        st = st_ref[hd]
        for c in range(n_chunks):
            sb_s[c, hd] = st.astype(BF16)
            st = st * chunk_decay[c][:, kcols] + kv_s[c, hd]
        st_ref[hd] = st

    for c in range(n_chunks):
        rows = slice(c * cs, (c + 1) * cs)
        for hd, (kcols, vcols) in enumerate(heads):
            o_s[rows, vcols] += _dot_nt(qt_s[rows, kcols], sb_s[c, hd])

    gate_out = _silu(_dot(hb, wg_ref[...]))
    outs = []
    for hd in range(GLA_HEADS):
        vcols = slice(hd * dv, (hd + 1) * dv)
        oh = o_s[:, vcols]
        oh = oh * lax.rsqrt(jnp.mean(oh * oh, -1, keepdims=True) + LN_EPS) * hng_ref[...]
        outs.append((oh * gate_out[:, vcols]).astype(BF16))
    m = _dot(jnp.concatenate(outs, -1), wout_ref[...])
    o_ref[0] = _layer_norm(alpha * h + m, g_ref[...], b_ref[...])


def _gla_mixer(h, w_in, w_gate2, b_gate, head_norm_g, w_out, g, b, alpha):
    bsz, l, d = h.shape
    dkt = w_gate2.shape[1]
    dvt = w_out.shape[0]
    s1, s2, s3 = dkt, 2 * dkt, 2 * dkt + dvt
    s4 = s3 + dvt
    wb = w_in.astype(BF16)
    wq, wk, wv, wg = wb[:, :s1], wb[:, s1:s2], wb[:, s2:s3], wb[:, s3:s4]
    wlow = jnp.pad(wb[:, s4:], ((0, 0), (0, LANES - GLA_RANK)))
    wgate2 = jnp.pad(w_gate2.astype(BF16), ((0, LANES - GLA_RANK), (0, 0)))
    dk = dkt // GLA_HEADS
    dv = dvt // GLA_HEADS
    tl = ROW_TILE
    row = pl.BlockSpec((1, tl, d), lambda bi, li: (bi, li, 0))
    return pl.pallas_call(
        functools.partial(_gla_kernel, alpha=alpha),
        grid=(bsz, l // tl),
        in_specs=[row, _const_spec((d, dkt)), _const_spec((d, dkt)), _const_spec((d, dvt)), _const_spec((d, dvt)),
                  _const_spec((d, LANES)), _const_spec((LANES, dkt)), _const_spec((1, dkt)), _const_spec((1, dv)),
                  _const_spec((dvt, d)), _const_spec((1, d)), _const_spec((1, d))],
        out_specs=row,
        out_shape=jax.ShapeDtypeStruct((bsz, l, d), F32),
        scratch_shapes=[pltpu.VMEM((GLA_HEADS, dv, dk), F32),
                        pltpu.VMEM((tl, dkt), F32), pltpu.VMEM((tl, dkt), F32), pltpu.VMEM((tl, dvt), BF16),
                        pltpu.VMEM((tl, dkt), F32), pltpu.VMEM((tl, dvt), F32), pltpu.VMEM((tl, dkt), BF16),
                        pltpu.VMEM((tl // GLA_CHUNK, GLA_HEADS, dv, dk), F32),
                        pltpu.VMEM((tl // GLA_CHUNK, GLA_HEADS, dv, dk), BF16)],
        compiler_params=_params(2),
        name="gla_mixer",
    )(h, wq, wk, wv, wg, wlow, wgate2, b_gate.reshape(1, dkt), head_norm_g.reshape(1, dv),
      w_out.astype(BF16), g, b)


def _causal_conv_block(buf_ref, w_ref, halo, r0, rb, cols):
    width = w_ref.shape[0]
    base = halo - (width - 1)
    n = rb + halo
    win = buf_ref[pl.ds(r0, n), cols]
    acc = None
    for b in range(SUBLANES):
        taps = [k for k in range(width) if (base + k) % SUBLANES == b]
        if not taps:
            continue
        wb = win if b == 0 else pltpu.roll(win, n - b, 0)
        for k in taps:
            a = (base + k) // SUBLANES
            term = w_ref[k:k + 1, cols] * wb[SUBLANES * a:SUBLANES * a + rb, :]
            acc = term if acc is None else acc + term
    return acc


def _even_kernel(h_ref, wval_ref, wgate_ref, wz_ref, wxbc_ref, wdt_ref,
                 cw_ref, cb_ref, clg_ref, clb_ref, sw_ref, sb_ref, dtb_ref, alog_ref, dskip_ref, sng_ref,
                 wou_ref, woy_ref, g_ref, b_ref, o_ref,
                 ubuf, xbuf, xs_s, b_s, c_s, acol_s, y_s, u_s, cv_s, st_ref, sc_s, sb_s, *, alpha):
    tl = h_ref.shape[1]
    d_inner = xs_s.shape[1]
    gn = SSM_GROUPS * SSM_STATE
    hpg_cols = d_inner // SSM_GROUPS
    n_heads = d_inner // SSM_HEAD_DIM
    q = SSM_CHUNK

    @pl.when(pl.program_id(1) == 0)
    def _():
        ubuf[0:CONV_HALO, :] = jnp.zeros((CONV_HALO, ubuf.shape[1]), F32)
        xbuf[0:SSM_CONV_HALO, :] = jnp.zeros((SSM_CONV_HALO, xbuf.shape[1]), F32)
        st_ref[...] = jnp.zeros_like(st_ref)

    h = h_ref[0]
    hb = h.astype(BF16)

    rb = CONV_ROWS

    for i in range(tl // rb):
        r0 = i * rb
        rows = pl.ds(r0, rb)
        hbi = hb[r0:r0 + rb, :]
        ubuf[CONV_HALO + r0:CONV_HALO + r0 + rb, :] = (_dot(hbi, wval_ref[...])
                                                      * _sigmoid(_dot(hbi, wgate_ref[...])))
        for c0 in range(0, cv_s.shape[1], LANES):
            cols = slice(c0, c0 + LANES)
            cv_s[rows, cols] = _causal_conv_block(ubuf, cw_ref, CONV_HALO, r0, rb, cols) + cb_ref[:, cols]
        u = _layer_norm(cv_s[rows, :], clg_ref[...], clb_ref[...])
        u_s[rows, :] = _silu(u).astype(BF16)
    ubuf[0:CONV_HALO, :] = ubuf[tl:tl + CONV_HALO, :]

    for i in range(tl // rb):
        r0 = i * rb
        rows = pl.ds(r0, rb)
        xbuf[SSM_CONV_HALO + r0:SSM_CONV_HALO + r0 + rb, :] = _dot(hb[r0:r0 + rb, :], wxbc_ref[...])
        for c0 in range(0, xbuf.shape[1], LANES):
            cols = slice(c0, c0 + LANES)
            v = _silu(_causal_conv_block(xbuf, sw_ref, SSM_CONV_HALO, r0, rb, cols) + sb_ref[:, cols])
            if c0 < d_inner:
                xs_s[rows, cols] = v
            elif c0 < d_inner + gn:
                b_s[rows, c0 - d_inner:c0 - d_inner + LANES] = v
            else:
                c_s[rows, c0 - d_inner - gn:c0 - d_inner - gn + LANES] = v
    xbuf[0:SSM_CONV_HALO, :] = xbuf[tl:tl + SSM_CONV_HALO, :]

    dt = _softplus(_dot(hb, wdt_ref[...]) + dtb_ref[...])
    a_neg = -jnp.exp(alog_ref[...])
    acol_s[...] = dt * a_neg
    hh = lax.broadcasted_iota(jnp.int32, (LANES, d_inner), 0)
    cc = lax.broadcasted_iota(jnp.int32, (LANES, d_inner), 1)
    expand = (cc // SSM_HEAD_DIM == hh).astype(BF16)
    y_s[...] = xs_s[...] * _expand_lanes(dt, expand)

    tri = _tri_incl(q)
    causal = tri > 0.0
    lane = lax.broadcasted_iota(jnp.int32, (q, LANES), 1)
    left = lane < SSM_HEAD_DIM

    n_chunks = tl // q
    groups = [(slice(gi * SSM_STATE, (gi + 1) * SSM_STATE), slice(gi * hpg_cols, (gi + 1) * hpg_cols))
              for gi in range(SSM_GROUPS)]

    chunk_decay = []
    for c in range(n_chunks):
        rows = slice(c * q, (c + 1) * q)
        acs = _cumsum_rows(tri, acol_s[rows, :])
        acol_s[rows, :] = acs
        a_last = acs[q - 1:q, :]
        factors = jnp.concatenate([jnp.exp(a_last - acs), jnp.broadcast_to(jnp.exp(a_last), (SUBLANES, LANES))], 0)
        factors = _expand_lanes(factors, expand)
        chunk_decay.append(factors[q:q + 1])
        xend_b = (y_s[rows, :] * factors[0:q]).astype(BF16)
        for gi, (ncols, gcols) in enumerate(groups):
            sc_s[c, gi] = _dot(b_s[rows, ncols].T.astype(BF16), xend_b[:, gcols])

    for gi, (_, gcols) in enumerate(groups):
        st = st_ref[gi]
        for c in range(n_chunks):
            sb_s[c, gi] = st.astype(BF16)
            st = st * chunk_decay[c][:, gcols] + sc_s[c, gi]
        st_ref[gi] = st

    for c in range(n_chunks):
        rows = slice(c * q, (c + 1) * q)
        acs = acol_s[rows, :]
        acs_t = acs.T
        xdt_b = y_s[rows, :].astype(BF16)
        for gi, (ncols, gcols) in enumerate(groups):
            cg = c_s[rows, ncols]
            cb = _dot_nt(cg.astype(BF16), b_s[rows, ncols].astype(BF16))
            for pair in range(hpg_cols // LANES):
                col0 = gi * hpg_cols + pair * LANES
                xp = xdt_b[:, col0:col0 + LANES]
                sp = sb_s[c, gi, :, pair * LANES:(pair + 1) * LANES]
                yp = None
                for half in range(LANES // SSM_HEAD_DIM):
                    hd = col0 // SSM_HEAD_DIM + half
                    a_l = jnp.broadcast_to(acs[:, hd:hd + 1], (q, q))
                    m_h = cb * jnp.exp(jnp.where(causal, a_l - acs_t[hd:hd + 1, :], -jnp.inf))
                    lhs = jnp.concatenate([m_h.astype(BF16), (cg * jnp.exp(a_l)).astype(BF16)], 1)
                    keep = left if half == 0 else jnp.logical_not(left)
                    rhs = jnp.concatenate([jnp.where(keep, xp, jnp.zeros_like(xp)),
                                           jnp.where(keep, sp, jnp.zeros_like(sp))], 0)
                    part = _dot(lhs, rhs)
                    yp = part if yp is None else yp + part
                y_s[rows, col0:col0 + LANES] = yp

    z = _dot(hb, wz_ref[...])
    y = (y_s[...] + xs_s[...] * dskip_ref[...]) * _silu(z)
    outs = []
    for gi in range(SSM_GROUPS):
        gcols = slice(gi * hpg_cols, (gi + 1) * hpg_cols)
        yg = y[:, gcols]
        yg = yg * lax.rsqrt(jnp.mean(yg * yg, -1, keepdims=True) + LN_EPS) * sng_ref[:, gcols]
        outs.append(yg.astype(BF16))
    m = _dot(u_s[...], wou_ref[...]) + _dot(jnp.concatenate(outs, -1), woy_ref[...])
    o_ref[0] = _layer_norm(alpha * h + m, g_ref[...], b_ref[...])


def _even_mixer(h, w_in, conv_w, conv_b, conv_ln_g, conv_ln_b, ssm_conv_w, ssm_conv_b, dt_bias, a_log,
                d_skip, ssm_norm_g, w_out, g, b, alpha):
    bsz, l, d = h.shape
    conv_ch = conv_w.shape[1]
    xbc_w = ssm_conv_w.shape[1]
    n_heads = a_log.shape[0]
    d_inner = n_heads * SSM_HEAD_DIM
    gn = SSM_GROUPS * SSM_STATE
    assert xbc_w == d_inner + 2 * gn
    c0, c1 = conv_ch, 2 * conv_ch
    c2 = c1 + d_inner
    c3 = c2 + xbc_w
    wb = w_in.astype(BF16)
    wval, wgate, wz, wxbc = wb[:, :c0], wb[:, c0:c1], wb[:, c1:c2], wb[:, c2:c3]
    pad_h = LANES - n_heads
    wdt = jnp.pad(wb[:, c3:], ((0, 0), (0, pad_h)))
    dtb = jnp.pad(dt_bias, (0, pad_h)).reshape(1, LANES)
    alog = jnp.pad(a_log, (0, pad_h)).reshape(1, LANES)
    dskip = jnp.repeat(d_skip, SSM_HEAD_DIM).reshape(1, d_inner)
    wo = w_out.astype(BF16)
    tl = ROW_TILE
    row = pl.BlockSpec((1, tl, d), lambda bi, li: (bi, li, 0))
    cs = _const_spec
    return pl.pallas_call(
        functools.partial(_even_kernel, alpha=alpha),
        grid=(bsz, l // tl),
        in_specs=[row, cs((d, conv_ch)), cs((d, conv_ch)), cs((d, d_inner)), cs((d, xbc_w)), cs((d, LANES)),
                  cs(conv_w.shape), cs((1, conv_ch)), cs((1, conv_ch)), cs((1, conv_ch)),
                  cs(ssm_conv_w.shape), cs((1, xbc_w)), cs((1, LANES)), cs((1, LANES)), cs((1, d_inner)),
                  cs((1, d_inner)), cs((conv_ch, d)), cs((d_inner, d)), cs((1, d)), cs((1, d))],
        out_specs=row,
        out_shape=jax.ShapeDtypeStruct((bsz, l, d), F32),
        scratch_shapes=[pltpu.VMEM((tl + CONV_HALO, conv_ch), F32),
                        pltpu.VMEM((tl + SSM_CONV_HALO, xbc_w), F32),
                        pltpu.VMEM((tl, d_inner), F32), pltpu.VMEM((tl, gn), F32), pltpu.VMEM((tl, gn), F32),
                        pltpu.VMEM((tl, LANES), F32), pltpu.VMEM((tl, d_inner), F32),
                        pltpu.VMEM((tl, conv_ch), BF16), pltpu.VMEM((tl, conv_ch), F32),
                        pltpu.VMEM((SSM_GROUPS, SSM_STATE, d_inner // SSM_GROUPS), F32),
                        pltpu.VMEM((tl // SSM_CHUNK, SSM_GROUPS, SSM_STATE, d_inner // SSM_GROUPS), F32),
                        pltpu.VMEM((tl // SSM_CHUNK, SSM_GROUPS, SSM_STATE, d_inner // SSM_GROUPS), BF16)],
        compiler_params=_params(2),
        name="conv_ssd_mixer",
    )(h, wval, wgate, wz, wxbc, wdt, conv_w, conv_b.reshape(1, -1), conv_ln_g.reshape(1, -1),
      conv_ln_b.reshape(1, -1), ssm_conv_w, ssm_conv_b.reshape(1, -1), dtb, alog, dskip,
      ssm_norm_g.reshape(1, -1), wo[:conv_ch], wo[conv_ch:], g, b)


def kernel(x, mem, even_w_in, even_conv_w, even_conv_b, even_conv_ln_g, even_conv_ln_b, even_ssm_conv_w,
           even_ssm_conv_b, even_dt_bias, even_a_log, even_d_skip, even_ssm_norm_g, even_w_out, odd_w_in,
           odd_w_gate2, odd_b_gate, odd_head_norm_g, odd_w_out, xa_w_q, xa_w_k, xa_w_v, xa_w_o, mlp_w1,
           mlp_w2, ln_g, ln_b):
    bsz, l, d = x.shape
    depth = ln_g.shape[0]
    alpha = float((2 * depth) ** 0.25)
    kt_all, v_all = _kv(mem, xa_w_k.astype(BF16), xa_w_v.astype(BF16))
    h = x
    for i in range(depth):
        j = i // 2
        g = [ln_g[i, s].reshape(1, d) for s in range(3)]
        b = [ln_b[i, s].reshape(1, d) for s in range(3)]
        if i % 2 == 0:
            h = _even_mixer(h, even_w_in[j], even_conv_w[j], even_conv_b[j], even_conv_ln_g[j],
                            even_conv_ln_b[j], even_ssm_conv_w[j], even_ssm_conv_b[j], even_dt_bias[j],
                            even_a_log[j], even_d_skip[j], even_ssm_norm_g[j], even_w_out[j], g[0], b[0], alpha)
        else:
            h = _gla_mixer(h, odd_w_in[j], odd_w_gate2[j], odd_b_gate[j], odd_head_norm_g[j], odd_w_out[j],
                           g[0], b[0], alpha)
        h = _xattn(h, kt_all[i], v_all[i], xa_w_q[i].astype(BF16), xa_w_o[i].astype(BF16), g[1], b[1], alpha)
        h = _mlp(h.reshape(bsz * l, d), mlp_w1[i].astype(BF16), mlp_w2[i].astype(BF16), g[2], b[2],
                 alpha).reshape(bsz, l, d)
    return h
```

```python
import functools

import jax
import jax.numpy as jnp
from jax import lax
from jax.experimental import pallas as pl
from jax.experimental.pallas import tpu as pltpu

F32 = jnp.float32
BF16 = jnp.bfloat16

LN_EPS = 1e-5
LANES = 128
VMEM_LIMIT_BYTES = 56 * 1024 * 1024

SSM_HEAD_DIM = 64
SSM_GROUPS = 2
SSM_STATE = 128
SSM_CHUNK = 128
GLA_HEADS = 4
GLA_RANK = 16
GLA_TAU = 16.0
GLA_CHUNK = 64
XA_HEADS = 4
CONV_HALO = 32
SSM_CONV_HALO = 8
SUBLANES = 8
CONV_ROWS = 128

EVEN_TILE = 512
GLA_TILE = 1024
XATTN_TILE = 1024
MLP_TILE = 1024
MLP_CHUNKS = 4


def _dot(a, b):
    return jnp.dot(a, b, preferred_element_type=F32)


def _dot_nt(a, b):
    return lax.dot_general(a, b, (((1,), (1,)), ((), ())), preferred_element_type=F32)


def _dot_tn(a, b):
    return lax.dot_general(a, b, (((0,), (0,)), ((), ())), preferred_element_type=F32)


def _layer_norm(x, g, b):
    mu = jnp.mean(x, -1, keepdims=True)
    xc = x - mu
    var = jnp.mean(xc * xc, -1, keepdims=True)
    return xc * lax.rsqrt(var + LN_EPS) * g + b


def _sigmoid(x):
    return 0.5 * jnp.tanh(0.5 * x) + 0.5


def _silu(x):
    hx = 0.5 * x
    return hx * jnp.tanh(hx) + hx


def _softplus(x):
    return jnp.maximum(x, 0.0) + jnp.log1p(jnp.exp(-jnp.abs(x)))


def _tri_incl(n):
    r = lax.broadcasted_iota(jnp.int32, (n, n), 0)
    c = lax.broadcasted_iota(jnp.int32, (n, n), 1)
    return c <= r


def _cumsum_rows(tri, x):
    return jnp.dot(tri, x, preferred_element_type=F32, precision=lax.Precision.HIGHEST)


def _expand_lanes(x, e):
    hi = x.astype(BF16)
    lo = (x - hi.astype(F32)).astype(BF16)
    return _dot(hi, e) + _dot(lo, e)


def _const_spec(shape):
    zeros = (0,) * len(shape)
    return pl.BlockSpec(shape, lambda *_: zeros, pipeline_mode=pl.Buffered(1))


def _params(n_grid):
    return pltpu.CompilerParams(dimension_semantics=("arbitrary",) * n_grid,
                                vmem_limit_bytes=VMEM_LIMIT_BYTES)


def _mlp_kernel(h_ref, w1_ref, w2_ref, g_ref, b_ref, o_ref, *, alpha):
    h = h_ref[...]
    hb = h.astype(BF16)
    cw = w1_ref.shape[1] // MLP_CHUNKS
    acc = None
    for j in range(MLP_CHUNKS):
        a = jnp.maximum(_dot(hb, w1_ref[:, j * cw:(j + 1) * cw]), 0.0)
        p = _dot((a * a).astype(BF16), w2_ref[j * cw:(j + 1) * cw, :])
        acc = p if acc is None else acc + p
    o_ref[...] = _layer_norm(alpha * h + acc, g_ref[...], b_ref[...])


def _mlp(h, w1, w2, g, b, alpha):
    t, d = h.shape
    dff = w1.shape[1]
    row = pl.BlockSpec((MLP_TILE, d), lambda i: (i, 0))
    return pl.pallas_call(
        functools.partial(_mlp_kernel, alpha=alpha),
        grid=(t // MLP_TILE,),
        in_specs=[row, _const_spec((d, dff)), _const_spec((dff, d)), _const_spec((1, d)), _const_spec((1, d))],
        out_specs=row,
        out_shape=jax.ShapeDtypeStruct((t, d), F32),
        compiler_params=_params(1),
        name="mlp",
    )(h, w1, w2, g, b)


def _kv_kernel(mem_ref, wk_ref, wv_ref, kt_ref, v_ref):
    mb = mem_ref[0].astype(BF16)
    kt_ref[0, 0] = _dot(mb, wk_ref[0]).T.astype(BF16)
    v_ref[0, 0] = _dot(mb, wv_ref[0]).astype(BF16)


def _kv(mem, wk, wv):
    bsz, m, d = mem.shape
    depth = wk.shape[0]
    w_spec = pl.BlockSpec((1, d, d), lambda i, b: (i, 0, 0))
    return pl.pallas_call(
        _kv_kernel,
        grid=(depth, bsz),
        in_specs=[pl.BlockSpec((1, m, d), lambda i, b: (b, 0, 0)), w_spec, w_spec],
        out_specs=[pl.BlockSpec((1, 1, d, m), lambda i, b: (i, b, 0, 0)),
                   pl.BlockSpec((1, 1, m, d), lambda i, b: (i, b, 0, 0))],
        out_shape=[jax.ShapeDtypeStruct((depth, bsz, d, m), BF16),
                   jax.ShapeDtypeStruct((depth, bsz, m, d), BF16)],
        compiler_params=_params(2),
        name="kv_proj",
    )(mem, wk, wv)


def _xattn_kernel(h_ref, kt_ref, v_ref, wq_ref, wo_ref, g_ref, b_ref, o_ref, *, alpha):
    h = h_ref[0]
    d = h.shape[-1]
    dh = d // XA_HEADS
    q = (_dot(h.astype(BF16), wq_ref[...]) * (dh ** -0.5)).astype(BF16)
    outs = []
    for hd in range(XA_HEADS):
        s = _dot(q[:, hd * dh:(hd + 1) * dh], kt_ref[0, hd * dh:(hd + 1) * dh, :])
        e = jnp.exp(s - jnp.max(s, -1, keepdims=True))
        p = e / jnp.sum(e, -1, keepdims=True)
        outs.append(_dot(p.astype(BF16), v_ref[0, :, hd * dh:(hd + 1) * dh]).astype(BF16))
    m = _dot(jnp.concatenate(outs, -1), wo_ref[...])
    o_ref[0] = _layer_norm(alpha * h + m, g_ref[...], b_ref[...])


def _xattn(h, kt, v, wq, wo, g, b, alpha):
    bsz, l, d = h.shape
    m = v.shape[1]
    row = pl.BlockSpec((1, XATTN_TILE, d), lambda bi, li: (bi, li, 0))
    return pl.pallas_call(
        functools.partial(_xattn_kernel, alpha=alpha),
        grid=(bsz, l // XATTN_TILE),
        in_specs=[row,
                  pl.BlockSpec((1, d, m), lambda bi, li: (bi, 0, 0)),
                  pl.BlockSpec((1, m, d), lambda bi, li: (bi, 0, 0)),
                  _const_spec((d, d)), _const_spec((d, d)), _const_spec((1, d)), _const_spec((1, d))],
        out_specs=row,
        out_shape=jax.ShapeDtypeStruct((bsz, l, d), F32),
        compiler_params=_params(2),
        name="xattn",
    )(h, kt, v, wq, wo, g, b)


def _gla_kernel(h_ref, win_ref, wgate2_ref, bgate_ref, hng_ref,
                wout_ref, g_ref, b_ref, o_ref,
                st_ref, q_s, k_s, v_s, bcs_s, o_s, qt_s, kv_s, sb_s, *, alpha):
    @pl.when(pl.program_id(1) == 0)
    def _():
        st_ref[...] = jnp.zeros_like(st_ref)

    h = h_ref[0]
    hb = h.astype(BF16)
    tl = h.shape[0]
    dk = q_s.shape[1] // GLA_HEADS
    dv = v_s.shape[1] // GLA_HEADS
    cs = GLA_CHUNK

    dkt, dvt = q_s.shape[1], v_s.shape[1]
    c_v, c_g = 2 * dkt, 2 * dkt + dvt
    c_low = c_g + dvt
    q_s[...] = _dot(hb, win_ref[:, 0:dkt])
    k_s[...] = _dot(hb, win_ref[:, dkt:c_v])
    v_s[...] = _dot(hb, win_ref[:, c_v:c_g]).astype(BF16)
    g_low = _dot(hb, win_ref[:, c_low:c_low + LANES])
    gate = _dot(g_low.astype(BF16), wgate2_ref[...]) + bgate_ref[...]
    bcs_s[...] = -_softplus(-gate) * (1.0 / GLA_TAU)

    causal = _tri_incl(cs)
    tri = causal.astype(F32)

    n_chunks = tl // cs
    heads = [(slice(hd * dk, (hd + 1) * dk), slice(hd * dv, (hd + 1) * dv)) for hd in range(GLA_HEADS)]

    chunk_rows = [slice(c * cs, (c + 1) * cs) for c in range(n_chunks)]
    b_wide = _cumsum_rows(tri, jnp.concatenate([bcs_s[r, :] for r in chunk_rows], 1))
    q_t, k_t, k_end, chunk_decay = [], [], [], []
    for c, r in enumerate(chunk_rows):
        b_all = b_wide[:, c * dkt:(c + 1) * dkt]
        b_last = b_all[cs - 1:cs, :]
        kc = k_s[r, :]
        q_t.append((q_s[r, :] * jnp.exp(b_all) * (dk ** -0.5)).astype(BF16))
        k_t.append((kc * jnp.exp(-b_all)).astype(BF16))
        k_end.append((kc * jnp.exp(b_last - b_all)).astype(BF16))
        chunk_decay.append(jnp.exp(b_last))
        qt_s[r, :] = q_t[c]
    att = [[jnp.where(causal, _dot_nt(q_t[c][:, kcols], k_t[c][:, kcols]), 0.0).astype(BF16)
            for kcols, _ in heads] for c in range(n_chunks)]
    for c, r in enumerate(chunk_rows):
        for hd, (kcols, vcols) in enumerate(heads):
            o_s[r, vcols] = _dot(att[c][hd], v_s[r, vcols])
    for c, r in enumerate(chunk_rows):
        for hd, (kcols, vcols) in enumerate(heads):
            kv_s[c, hd] = _dot_tn(v_s[r, vcols], k_end[c][:, kcols])

    for hd, (kcols, _) in enumerate(heads):
        st = st_ref[hd]
        for c in range(n_chunks):
            sb_s[c, hd] = st.astype(BF16)
            st = st * chunk_decay[c][:, kcols] + kv_s[c, hd]
        st_ref[hd] = st

    for c in range(n_chunks):
        rows = slice(c * cs, (c + 1) * cs)
        for hd, (kcols, vcols) in enumerate(heads):
            o_s[rows, vcols] += _dot_nt(qt_s[rows, kcols], sb_s[c, hd])

    gate_out = _silu(_dot(hb, win_ref[:, c_g:c_low]))
    outs = []
    for hd in range(GLA_HEADS):
        vcols = slice(hd * dv, (hd + 1) * dv)
        oh = o_s[:, vcols]
        oh = oh * lax.rsqrt(jnp.mean(oh * oh, -1, keepdims=True) + LN_EPS) * hng_ref[...]
        outs.append((oh * gate_out[:, vcols]).astype(BF16))
    m = _dot(jnp.concatenate(outs, -1), wout_ref[...])
    o_ref[0] = _layer_norm(alpha * h + m, g_ref[...], b_ref[...])


def _gla_mixer(h, w_in, w_gate2, b_gate, head_norm_g, w_out, g, b, alpha):
    bsz, l, d = h.shape
    dkt = w_gate2.shape[1]
    dvt = w_out.shape[0]
    assert w_in.shape[1] == 2 * dkt + 2 * dvt + GLA_RANK
    win = jnp.pad(w_in.astype(BF16), ((0, 0), (0, LANES - GLA_RANK)))
    wgate2 = jnp.pad(w_gate2.astype(BF16), ((0, LANES - GLA_RANK), (0, 0)))
    dk = dkt // GLA_HEADS
    dv = dvt // GLA_HEADS
    tl = GLA_TILE
    row = pl.BlockSpec((1, tl, d), lambda bi, li: (bi, li, 0))
    return pl.pallas_call(
        functools.partial(_gla_kernel, alpha=alpha),
        grid=(bsz, l // tl),
        in_specs=[row, _const_spec(win.shape), _const_spec((LANES, dkt)), _const_spec((1, dkt)), _const_spec((1, dv)),
                  _const_spec((dvt, d)), _const_spec((1, d)), _const_spec((1, d))],
        out_specs=row,
        out_shape=jax.ShapeDtypeStruct((bsz, l, d), F32),
        scratch_shapes=[pltpu.VMEM((GLA_HEADS, dv, dk), F32),
                        pltpu.VMEM((tl, dkt), F32), pltpu.VMEM((tl, dkt), F32), pltpu.VMEM((tl, dvt), BF16),
                        pltpu.VMEM((tl, dkt), F32), pltpu.VMEM((tl, dvt), F32), pltpu.VMEM((tl, dkt), BF16),
                        pltpu.VMEM((tl // GLA_CHUNK, GLA_HEADS, dv, dk), F32),
                        pltpu.VMEM((tl // GLA_CHUNK, GLA_HEADS, dv, dk), BF16)],
        compiler_params=_params(2),
        name="gla_mixer",
    )(h, win, wgate2, b_gate.reshape(1, dkt), head_norm_g.reshape(1, dv),
      w_out.astype(BF16), g, b)


def _causal_conv_block(buf_ref, w_ref, halo, r0, rb, cols):
    width = w_ref.shape[0]
    base = halo - (width - 1)
    n = rb + halo
    win = buf_ref[pl.ds(r0, n), cols]
    acc = None
    for b in range(SUBLANES):
        taps = [k for k in range(width) if (base + k) % SUBLANES == b]
        if not taps:
            continue
        wb = win if b == 0 else pltpu.roll(win, n - b, 0)
        for k in taps:
            a = (base + k) // SUBLANES
            term = w_ref[k:k + 1, cols] * wb[SUBLANES * a:SUBLANES * a + rb, :]
            acc = term if acc is None else acc + term
    return acc


def _even_kernel(h_ref, win_ref,
                 cw_ref, cb_ref, clg_ref, clb_ref, sw_ref, sb_ref, dtb_ref, alog_ref, dskip_ref, sng_ref,
                 wo_ref, g_ref, b_ref, o_ref,
                 ubuf, xbuf, xs_s, b_s, c_s, acol_s, y_s, u_s, cv_s, st_ref, sc_s, sb_s, *, alpha):
    tl = h_ref.shape[1]
    d_inner = xs_s.shape[1]
    gn = SSM_GROUPS * SSM_STATE
    hpg_cols = d_inner // SSM_GROUPS
    n_heads = d_inner // SSM_HEAD_DIM
    q = SSM_CHUNK
    conv_ch = cw_ref.shape[1]
    c_gate, c_z = conv_ch, 2 * conv_ch
    c_xbc = c_z + d_inner
    c_dt = c_xbc + xbuf.shape[1]

    @pl.when(pl.program_id(1) == 0)
    def _():
        ubuf[0:CONV_HALO, :] = jnp.zeros((CONV_HALO, ubuf.shape[1]), F32)
        xbuf[0:SSM_CONV_HALO, :] = jnp.zeros((SSM_CONV_HALO, xbuf.shape[1]), F32)
        st_ref[...] = jnp.zeros_like(st_ref)

    h = h_ref[0]
    hb = h.astype(BF16)

    rb = CONV_ROWS
    for i in range(tl // rb):
        r0 = i * rb
        rows = pl.ds(r0, rb)
        hbi = hb[r0:r0 + rb, :]
        ubuf[CONV_HALO + r0:CONV_HALO + r0 + rb, :] = (_dot(hbi, win_ref[:, 0:c_gate])
                                                      * _sigmoid(_dot(hbi, win_ref[:, c_gate:c_z])))
        for c0 in range(0, cv_s.shape[1], LANES):
            cols = slice(c0, c0 + LANES)
            cv_s[rows, cols] = _causal_conv_block(ubuf, cw_ref, CONV_HALO, r0, rb, cols) + cb_ref[:, cols]
        u = _layer_norm(cv_s[rows, :], clg_ref[...], clb_ref[...])
        u_s[rows, :] = _silu(u).astype(BF16)
    ubuf[0:CONV_HALO, :] = ubuf[tl:tl + CONV_HALO, :]

    for i in range(tl // rb):
        r0 = i * rb
        rows = pl.ds(r0, rb)
        xbuf[SSM_CONV_HALO + r0:SSM_CONV_HALO + r0 + rb, :] = _dot(hb[r0:r0 + rb, :], win_ref[:, c_xbc:c_dt])
        for c0 in range(0, xbuf.shape[1], LANES):
            cols = slice(c0, c0 + LANES)
            v = _silu(_causal_conv_block(xbuf, sw_ref, SSM_CONV_HALO, r0, rb, cols) + sb_ref[:, cols])
            if c0 < d_inner:
                xs_s[rows, cols] = v
            elif c0 < d_inner + gn:
                b_s[rows, c0 - d_inner:c0 - d_inner + LANES] = v
            else:
                c_s[rows, c0 - d_inner - gn:c0 - d_inner - gn + LANES] = v
    xbuf[0:SSM_CONV_HALO, :] = xbuf[tl:tl + SSM_CONV_HALO, :]

    dt = _softplus(_dot(hb, win_ref[:, c_dt:c_dt + LANES]) + dtb_ref[...])
    a_neg = -jnp.exp(alog_ref[...])
    acol_s[...] = dt * a_neg
    hh = lax.broadcasted_iota(jnp.int32, (LANES, d_inner), 0)
    cc = lax.broadcasted_iota(jnp.int32, (LANES, d_inner), 1)
    expand = (cc // SSM_HEAD_DIM == hh).astype(BF16)
    y_s[...] = xs_s[...] * _expand_lanes(dt, expand)

    causal = _tri_incl(q)
    tri = causal.astype(F32)
    lane = lax.broadcasted_iota(jnp.int32, (q, LANES), 1)
    left = lane < SSM_HEAD_DIM

    n_chunks = tl // q
    groups = [(slice(gi * SSM_STATE, (gi + 1) * SSM_STATE), slice(gi * hpg_cols, (gi + 1) * hpg_cols))
              for gi in range(SSM_GROUPS)]

    chunk_decay = []
    for c in range(n_chunks):
        rows = slice(c * q, (c + 1) * q)
        acs = _cumsum_rows(tri, acol_s[rows, :])
        acol_s[rows, :] = acs
        a_last = acs[q - 1:q, :]
        factors = jnp.concatenate([jnp.exp(a_last - acs), jnp.broadcast_to(jnp.exp(a_last), (SUBLANES, LANES))], 0)
        factors = _expand_lanes(factors, expand)
        chunk_decay.append(factors[q:q + 1])
        xend_b = (y_s[rows, :] * factors[0:q]).astype(BF16)
        for gi, (ncols, gcols) in enumerate(groups):
            sc_s[c, gi] = _dot(b_s[rows, ncols].T.astype(BF16), xend_b[:, gcols])

    for gi, (_, gcols) in enumerate(groups):
        st = st_ref[gi]
        for c in range(n_chunks):
            sb_s[c, gi] = st.astype(BF16)
            st = st * chunk_decay[c][:, gcols] + sc_s[c, gi]
        st_ref[gi] = st

    for c in range(n_chunks):
        rows = slice(c * q, (c + 1) * q)
        acs = acol_s[rows, :]
        acs_t = acs.T
        xdt_b = y_s[rows, :].astype(BF16)
        for gi, (ncols, gcols) in enumerate(groups):
            cg = c_s[rows, ncols]
            cb = _dot_nt(cg.astype(BF16), b_s[rows, ncols].astype(BF16))
            for pair in range(hpg_cols // LANES):
                col0 = gi * hpg_cols + pair * LANES
                xp = xdt_b[:, col0:col0 + LANES]
                sp = sb_s[c, gi, :, pair * LANES:(pair + 1) * LANES]
                yp = None
                for half in range(LANES // SSM_HEAD_DIM):
                    hd = col0 // SSM_HEAD_DIM + half
                    a_l = jnp.broadcast_to(acs[:, hd:hd + 1], (q, q))
                    m_h = cb * jnp.exp(jnp.where(causal, a_l - acs_t[hd:hd + 1, :], -jnp.inf))
                    lhs = jnp.concatenate([m_h.astype(BF16), (cg * jnp.exp(a_l)).astype(BF16)], 1)
                    keep = left if half == 0 else jnp.logical_not(left)
                    rhs = jnp.concatenate([jnp.where(keep, xp, jnp.zeros_like(xp)),
                                           jnp.where(keep, sp, jnp.zeros_like(sp))], 0)
                    part = _dot(lhs, rhs)
                    yp = part if yp is None else yp + part
                y_s[rows, col0:col0 + LANES] = yp

    z = _dot(hb, win_ref[:, c_z:c_xbc])
    y = (y_s[...] + xs_s[...] * dskip_ref[...]) * _silu(z)
    outs = []
    for gi in range(SSM_GROUPS):
        gcols = slice(gi * hpg_cols, (gi + 1) * hpg_cols)
        yg = y[:, gcols]
        yg = yg * lax.rsqrt(jnp.mean(yg * yg, -1, keepdims=True) + LN_EPS) * sng_ref[:, gcols]
        outs.append(yg.astype(BF16))
    m = _dot(u_s[...], wo_ref[0:conv_ch, :]) + _dot(jnp.concatenate(outs, -1), wo_ref[conv_ch:, :])
    o_ref[0] = _layer_norm(alpha * h + m, g_ref[...], b_ref[...])


def _even_mixer(h, w_in, conv_w, conv_b, conv_ln_g, conv_ln_b, ssm_conv_w, ssm_conv_b, dt_bias, a_log,
                d_skip, ssm_norm_g, w_out, g, b, alpha):
    bsz, l, d = h.shape
    conv_ch = conv_w.shape[1]
    xbc_w = ssm_conv_w.shape[1]
    n_heads = a_log.shape[0]
    d_inner = n_heads * SSM_HEAD_DIM
    gn = SSM_GROUPS * SSM_STATE
    assert xbc_w == d_inner + 2 * gn
    pad_h = LANES - n_heads
    assert w_in.shape[1] == 2 * conv_ch + d_inner + xbc_w + n_heads
    win = jnp.pad(w_in.astype(BF16), ((0, 0), (0, pad_h)))
    dtb = jnp.pad(dt_bias, (0, pad_h)).reshape(1, LANES)
    alog = jnp.pad(a_log, (0, pad_h)).reshape(1, LANES)
    dskip = jnp.repeat(d_skip, SSM_HEAD_DIM).reshape(1, d_inner)
    wo = w_out.astype(BF16)
    tl = EVEN_TILE
    row = pl.BlockSpec((1, tl, d), lambda bi, li: (bi, li, 0))
    cs = _const_spec
    state = (SSM_GROUPS, SSM_STATE, d_inner // SSM_GROUPS)
    scratch = [pltpu.VMEM((tl + CONV_HALO, conv_ch), F32),
               pltpu.VMEM((tl + SSM_CONV_HALO, xbc_w), F32),
               pltpu.VMEM((tl, d_inner), F32), pltpu.VMEM((tl, gn), F32), pltpu.VMEM((tl, gn), F32),
               pltpu.VMEM((tl, LANES), F32), pltpu.VMEM((tl, d_inner), F32),
               pltpu.VMEM((tl, conv_ch), BF16), pltpu.VMEM((tl, conv_ch), F32),
               pltpu.VMEM(state, F32),
               pltpu.VMEM((tl // SSM_CHUNK,) + state, F32), pltpu.VMEM((tl // SSM_CHUNK,) + state, BF16)]
    return pl.pallas_call(
        functools.partial(_even_kernel, alpha=alpha),
        grid=(bsz, l // tl),
        in_specs=[row, cs(win.shape),
                  cs(conv_w.shape), cs((1, conv_ch)), cs((1, conv_ch)), cs((1, conv_ch)),
                  cs(ssm_conv_w.shape), cs((1, xbc_w)), cs((1, LANES)), cs((1, LANES)), cs((1, d_inner)),
                  cs((1, d_inner)), cs(wo.shape), cs((1, d)), cs((1, d))],
        out_specs=row,
        out_shape=jax.ShapeDtypeStruct((bsz, l, d), F32),
        scratch_shapes=scratch,
        compiler_params=_params(2),
        name="conv_ssd_mixer",
    )(h, win, conv_w, conv_b.reshape(1, -1), conv_ln_g.reshape(1, -1),
      conv_ln_b.reshape(1, -1), ssm_conv_w, ssm_conv_b.reshape(1, -1), dtb, alog, dskip,
      ssm_norm_g.reshape(1, -1), wo, g, b)


def kernel(x, mem, even_w_in, even_conv_w, even_conv_b, even_conv_ln_g, even_conv_ln_b, even_ssm_conv_w,
           even_ssm_conv_b, even_dt_bias, even_a_log, even_d_skip, even_ssm_norm_g, even_w_out, odd_w_in,
           odd_w_gate2, odd_b_gate, odd_head_norm_g, odd_w_out, xa_w_q, xa_w_k, xa_w_v, xa_w_o, mlp_w1,
           mlp_w2, ln_g, ln_b):
    bsz, l, d = x.shape
    depth = ln_g.shape[0]
    alpha = float((2 * depth) ** 0.25)
    kt_all, v_all = _kv(mem, xa_w_k.astype(BF16), xa_w_v.astype(BF16))
    h = x
    for i in range(depth):
        j = i // 2
        g = [ln_g[i, s].reshape(1, d) for s in range(3)]
        b = [ln_b[i, s].reshape(1, d) for s in range(3)]
        if i % 2 == 0:
            h = _even_mixer(h, even_w_in[j], even_conv_w[j], even_conv_b[j], even_conv_ln_g[j],
                            even_conv_ln_b[j], even_ssm_conv_w[j], even_ssm_conv_b[j], even_dt_bias[j],
                            even_a_log[j], even_d_skip[j], even_ssm_norm_g[j], even_w_out[j], g[0], b[0], alpha)
        else:
            h = _gla_mixer(h, odd_w_in[j], odd_w_gate2[j], odd_b_gate[j], odd_head_norm_g[j], odd_w_out[j],
                           g[0], b[0], alpha)
        h = _xattn(h, kt_all[i], v_all[i], xa_w_q[i].astype(BF16), xa_w_o[i].astype(BF16), g[1], b[1], alpha)
        h = _mlp(h.reshape(bsz * l, d), mlp_w1[i].astype(BF16), mlp_w2[i].astype(BF16), g[2], b[2],
                 alpha).reshape(bsz, l, d)
    return h
```

```python
import functools

import jax
import jax.numpy as jnp
from jax import lax
from jax.experimental import pallas as pl
from jax.experimental.pallas import tpu as pltpu

F32 = jnp.float32
BF16 = jnp.bfloat16

LN_EPS = 1e-5
LANES = 128
VMEM_LIMIT_BYTES = 56 * 1024 * 1024

SSM_HEAD_DIM = 64
SSM_GROUPS = 2
SSM_STATE = 128
SSM_CHUNK = 128
GLA_HEADS = 4
GLA_RANK = 16
GLA_TAU = 16.0
GLA_CHUNK = 64
XA_HEADS = 4
CONV_HALO = 32
SSM_CONV_HALO = 8
SUBLANES = 8
CONV_ROWS = 128

EVEN_TILE = 512
GLA_TILE = 1024
XATTN_TILE = 2048
MLP_TILE = 1024
MLP_CHUNKS = 4


def _dot(a, b):
    return jnp.dot(a, b, preferred_element_type=F32)


def _dot_nt(a, b):
    return lax.dot_general(a, b, (((1,), (1,)), ((), ())), preferred_element_type=F32)


def _dot_tn(a, b):
    return lax.dot_general(a, b, (((0,), (0,)), ((), ())), preferred_element_type=F32)


def _layer_norm(x, g, b):
    mu = jnp.mean(x, -1, keepdims=True)
    xc = x - mu
    var = jnp.mean(xc * xc, -1, keepdims=True)
    return xc * lax.rsqrt(var + LN_EPS) * g + b


def _sigmoid(x):
    return 0.5 * jnp.tanh(0.5 * x) + 0.5


def _silu(x):
    hx = 0.5 * x
    return hx * jnp.tanh(hx) + hx


def _softplus(x):
    return jnp.maximum(x, 0.0) + jnp.log1p(jnp.exp(-jnp.abs(x)))


def _tri_incl(n):
    r = lax.broadcasted_iota(jnp.int32, (n, n), 0)
    c = lax.broadcasted_iota(jnp.int32, (n, n), 1)
    return c <= r


def _cumsum_rows(tri, x):
    return jnp.dot(tri, x, preferred_element_type=F32, precision=lax.Precision.HIGHEST)


def _expand_lanes(x, e):
    hi = x.astype(BF16)
    lo = (x - hi.astype(F32)).astype(BF16)
    return _dot(hi, e) + _dot(lo, e)


def _const_spec(shape):
    zeros = (0,) * len(shape)
    return pl.BlockSpec(shape, lambda *_: zeros, pipeline_mode=pl.Buffered(1))


def _params(n_grid):
    return pltpu.CompilerParams(dimension_semantics=("arbitrary",) * n_grid,
                                vmem_limit_bytes=VMEM_LIMIT_BYTES)


def _layer_spec(stacked, layer):
    shape = stacked.shape[1:]
    zeros = (0,) * len(shape)
    return pl.BlockSpec((None,) + shape, lambda *_: (layer,) + zeros, pipeline_mode=pl.Buffered(1))


def _mlp_kernel(h_ref, w1_ref, w2_ref, g_ref, b_ref, o_ref, *, alpha):
    h = h_ref[...]
    hb = h.astype(BF16)
    cw = w1_ref.shape[1] // MLP_CHUNKS
    acc = None
    for j in range(MLP_CHUNKS):
        a = jnp.maximum(_dot(hb, w1_ref[:, j * cw:(j + 1) * cw]), 0.0)
        p = _dot((a * a).astype(BF16), w2_ref[j * cw:(j + 1) * cw, :])
        acc = p if acc is None else acc + p
    o_ref[...] = _layer_norm(alpha * h + acc, g_ref[...], b_ref[...])


def _mlp(h, w1, w2, layer, g, b, alpha):
    t, d = h.shape
    row = pl.BlockSpec((MLP_TILE, d), lambda i: (i, 0))
    return pl.pallas_call(
        functools.partial(_mlp_kernel, alpha=alpha),
        grid=(t // MLP_TILE,),
        in_specs=[row, _layer_spec(w1, layer), _layer_spec(w2, layer), _const_spec((1, d)), _const_spec((1, d))],
        out_specs=row,
        out_shape=jax.ShapeDtypeStruct((t, d), F32),
        compiler_params=_params(1),
        name="mlp",
    )(h, w1, w2, g, b)


def _kv_kernel(mem_ref, wk_ref, wv_ref, kt_ref, v_ref):
    mb = mem_ref[0].astype(BF16)
    kt_ref[0, 0] = _dot(mb, wk_ref[0]).T.astype(BF16)
    v_ref[0, 0] = _dot(mb, wv_ref[0]).astype(BF16)


def _kv(mem, wk, wv):
    bsz, m, d = mem.shape
    depth = wk.shape[0]
    w_spec = pl.BlockSpec((1, d, d), lambda i, b: (i, 0, 0))
    return pl.pallas_call(
        _kv_kernel,
        grid=(depth, bsz),
        in_specs=[pl.BlockSpec((1, m, d), lambda i, b: (b, 0, 0)), w_spec, w_spec],
        out_specs=[pl.BlockSpec((1, 1, d, m), lambda i, b: (i, b, 0, 0)),
                   pl.BlockSpec((1, 1, m, d), lambda i, b: (i, b, 0, 0))],
        out_shape=[jax.ShapeDtypeStruct((depth, bsz, d, m), BF16),
                   jax.ShapeDtypeStruct((depth, bsz, m, d), BF16)],
        compiler_params=_params(2),
        name="kv_proj",
    )(mem, wk, wv)


def _xattn_kernel(h_ref, kt_ref, v_ref, wq_ref, wo_ref, g_ref, b_ref, o_ref, *, alpha):
    h = h_ref[0]
    d = h.shape[-1]
    dh = d // XA_HEADS
    q = (_dot(h.astype(BF16), wq_ref[...]) * (dh ** -0.5)).astype(BF16)
    outs = []
    for hd in range(XA_HEADS):
        s = _dot(q[:, hd * dh:(hd + 1) * dh], kt_ref[0, hd * dh:(hd + 1) * dh, :])
        e = jnp.exp(s - jnp.max(s, -1, keepdims=True))
        p = e / jnp.sum(e, -1, keepdims=True)
        outs.append(_dot(p.astype(BF16), v_ref[0, :, hd * dh:(hd + 1) * dh]).astype(BF16))
    m = _dot(jnp.concatenate(outs, -1), wo_ref[...])
    o_ref[0] = _layer_norm(alpha * h + m, g_ref[...], b_ref[...])


def _xattn(h, kt, v, wq, wo, layer, g, b, alpha):
    bsz, l, d = h.shape
    m = v.shape[2]
    row = pl.BlockSpec((1, XATTN_TILE, d), lambda bi, li: (bi, li, 0))
    return pl.pallas_call(
        functools.partial(_xattn_kernel, alpha=alpha),
        grid=(bsz, l // XATTN_TILE),
        in_specs=[row,
                  pl.BlockSpec((None, 1, d, m), lambda bi, li: (layer, bi, 0, 0)),
                  pl.BlockSpec((None, 1, m, d), lambda bi, li: (layer, bi, 0, 0)),
                  _layer_spec(wq, layer), _layer_spec(wo, layer), _const_spec((1, d)), _const_spec((1, d))],
        out_specs=row,
        out_shape=jax.ShapeDtypeStruct((bsz, l, d), F32),
        compiler_params=_params(2),
        name="xattn",
    )(h, kt, v, wq, wo, g, b)


def _gla_kernel(h_ref, win_ref, wgate2_ref, bgate_ref, hng_ref,
                wout_ref, g_ref, b_ref, o_ref,
                st_ref, q_s, k_s, v_s, bcs_s, o_s, qt_s, kv_s, sb_s, *, alpha):
    @pl.when(pl.program_id(1) == 0)
    def _():
        st_ref[...] = jnp.zeros_like(st_ref)

    h = h_ref[0]
    hb = h.astype(BF16)
    tl = h.shape[0]
    dk = q_s.shape[1] // GLA_HEADS
    dv = v_s.shape[1] // GLA_HEADS
    cs = GLA_CHUNK

    dkt, dvt = q_s.shape[1], v_s.shape[1]
    c_v, c_g = 2 * dkt, 2 * dkt + dvt
    c_low = c_g + dvt
    q_s[...] = _dot(hb, win_ref[:, 0:dkt])
    k_s[...] = _dot(hb, win_ref[:, dkt:c_v])
    v_s[...] = _dot(hb, win_ref[:, c_v:c_g]).astype(BF16)
    g_low = _dot(hb, win_ref[:, c_low:c_low + LANES])
    gate = _dot(g_low.astype(BF16), wgate2_ref[...]) + bgate_ref[...]
    bcs_s[...] = -_softplus(-gate) * (1.0 / GLA_TAU)

    causal = _tri_incl(cs)
    tri = causal.astype(F32)

    n_chunks = tl // cs
    heads = [(slice(hd * dk, (hd + 1) * dk), slice(hd * dv, (hd + 1) * dv)) for hd in range(GLA_HEADS)]

    chunk_rows = [slice(c * cs, (c + 1) * cs) for c in range(n_chunks)]
    b_wide = _cumsum_rows(tri, jnp.concatenate([bcs_s[r, :] for r in chunk_rows], 1))
    q_t, k_t, k_end, chunk_decay = [], [], [], []
    for c, r in enumerate(chunk_rows):
        b_all = b_wide[:, c * dkt:(c + 1) * dkt]
        b_last = b_all[cs - 1:cs, :]
        kc = k_s[r, :]
        q_t.append((q_s[r, :] * jnp.exp(b_all) * (dk ** -0.5)).astype(BF16))
        k_t.append((kc * jnp.exp(-b_all)).astype(BF16))
        k_end.append((kc * jnp.exp(b_last - b_all)).astype(BF16))
        chunk_decay.append(jnp.exp(b_last))
        qt_s[r, :] = q_t[c]
    att = [[jnp.where(causal, _dot_nt(q_t[c][:, kcols], k_t[c][:, kcols]), 0.0).astype(BF16)
            for kcols, _ in heads] for c in range(n_chunks)]
    for c, r in enumerate(chunk_rows):
        for hd, (kcols, vcols) in enumerate(heads):
            o_s[r, vcols] = _dot(att[c][hd], v_s[r, vcols])
    for c, r in enumerate(chunk_rows):
        for hd, (kcols, vcols) in enumerate(heads):
            kv_s[c, hd] = _dot_tn(v_s[r, vcols], k_end[c][:, kcols])

    for hd, (kcols, _) in enumerate(heads):
        st = st_ref[hd]
        for c in range(n_chunks):
            sb_s[c, hd] = st.astype(BF16)
            st = st * chunk_decay[c][:, kcols] + kv_s[c, hd]
        st_ref[hd] = st

    for c in range(n_chunks):
        rows = slice(c * cs, (c + 1) * cs)
        for hd, (kcols, vcols) in enumerate(heads):
            o_s[rows, vcols] += _dot_nt(qt_s[rows, kcols], sb_s[c, hd])

    gate_out = _silu(_dot(hb, win_ref[:, c_g:c_low]))
    outs = []
    for hd in range(GLA_HEADS):
        vcols = slice(hd * dv, (hd + 1) * dv)
        oh = o_s[:, vcols]
        oh = oh * lax.rsqrt(jnp.mean(oh * oh, -1, keepdims=True) + LN_EPS) * hng_ref[...]
        outs.append((oh * gate_out[:, vcols]).astype(BF16))
    m = _dot(jnp.concatenate(outs, -1), wout_ref[...])
    o_ref[0] = _layer_norm(alpha * h + m, g_ref[...], b_ref[...])


def _gla_mixer(h, w_in, w_gate2, b_gate, head_norm_g, w_out, g, b, alpha):
    bsz, l, d = h.shape
    dkt = w_gate2.shape[1]
    dvt = w_out.shape[0]
    assert w_in.shape[1] == 2 * dkt + 2 * dvt + GLA_RANK
    win = jnp.pad(w_in.astype(BF16), ((0, 0), (0, LANES - GLA_RANK)))
    wgate2 = jnp.pad(w_gate2.astype(BF16), ((0, LANES - GLA_RANK), (0, 0)))
    dk = dkt // GLA_HEADS
    dv = dvt // GLA_HEADS
    tl = GLA_TILE
    row = pl.BlockSpec((1, tl, d), lambda bi, li: (bi, li, 0))
    return pl.pallas_call(
        functools.partial(_gla_kernel, alpha=alpha),
        grid=(bsz, l // tl),
        in_specs=[row, _const_spec(win.shape), _const_spec((LANES, dkt)), _const_spec((1, dkt)), _const_spec((1, dv)),
                  _const_spec((dvt, d)), _const_spec((1, d)), _const_spec((1, d))],
        out_specs=row,
        out_shape=jax.ShapeDtypeStruct((bsz, l, d), F32),
        scratch_shapes=[pltpu.VMEM((GLA_HEADS, dv, dk), F32),
                        pltpu.VMEM((tl, dkt), F32), pltpu.VMEM((tl, dkt), F32), pltpu.VMEM((tl, dvt), BF16),
                        pltpu.VMEM((tl, dkt), F32), pltpu.VMEM((tl, dvt), F32), pltpu.VMEM((tl, dkt), BF16),
                        pltpu.VMEM((tl // GLA_CHUNK, GLA_HEADS, dv, dk), F32),
                        pltpu.VMEM((tl // GLA_CHUNK, GLA_HEADS, dv, dk), BF16)],
        compiler_params=_params(2),
        name="gla_mixer",
    )(h, win, wgate2, b_gate.reshape(1, dkt), head_norm_g.reshape(1, dv),
      w_out.astype(BF16), g, b)


def _causal_conv_block(buf_ref, w_ref, halo, r0, rb, cols):
    width = w_ref.shape[0]
    base = halo - (width - 1)
    n = rb + halo
    win = buf_ref[pl.ds(r0, n), cols]
    acc = None
    for b in range(SUBLANES):
        taps = [k for k in range(width) if (base + k) % SUBLANES == b]
        if not taps:
            continue
        wb = win if b == 0 else pltpu.roll(win, n - b, 0)
        for k in taps:
            a = (base + k) // SUBLANES
            term = w_ref[k:k + 1, cols] * wb[SUBLANES * a:SUBLANES * a + rb, :]
            acc = term if acc is None else acc + term
    return acc


def _even_kernel(h_ref, win_ref,
                 cw_ref, cb_ref, clg_ref, clb_ref, sw_ref, sb_ref, dtb_ref, alog_ref, dskip_ref, sng_ref,
                 wo_ref, g_ref, b_ref, o_ref,
                 ubuf, xbuf, xs_s, b_s, c_s, acol_s, y_s, u_s, cv_s, st_ref, sc_s, sb_s, *, alpha):
    tl = h_ref.shape[1]
    d_inner = xs_s.shape[1]
    gn = SSM_GROUPS * SSM_STATE
    hpg_cols = d_inner // SSM_GROUPS
    n_heads = d_inner // SSM_HEAD_DIM
    q = SSM_CHUNK
    conv_ch = cw_ref.shape[1]
    c_gate, c_z = conv_ch, 2 * conv_ch
    c_xbc = c_z + d_inner
    c_dt = c_xbc + xbuf.shape[1]

    @pl.when(pl.program_id(1) == 0)
    def _():
        ubuf[0:CONV_HALO, :] = jnp.zeros((CONV_HALO, ubuf.shape[1]), F32)
        xbuf[0:SSM_CONV_HALO, :] = jnp.zeros((SSM_CONV_HALO, xbuf.shape[1]), F32)
        st_ref[...] = jnp.zeros_like(st_ref)

    h = h_ref[0]
    hb = h.astype(BF16)

    rb = CONV_ROWS
    for i in range(tl // rb):
        r0 = i * rb
        rows = pl.ds(r0, rb)
        hbi = hb[r0:r0 + rb, :]
        ubuf[CONV_HALO + r0:CONV_HALO + r0 + rb, :] = (_dot(hbi, win_ref[:, 0:c_gate])
                                                      * _sigmoid(_dot(hbi, win_ref[:, c_gate:c_z])))
        for c0 in range(0, cv_s.shape[1], LANES):
            cols = slice(c0, c0 + LANES)
            cv_s[rows, cols] = _causal_conv_block(ubuf, cw_ref, CONV_HALO, r0, rb, cols) + cb_ref[:, cols]
        u = _layer_norm(cv_s[rows, :], clg_ref[...], clb_ref[...])
        u_s[rows, :] = _silu(u).astype(BF16)
    ubuf[0:CONV_HALO, :] = ubuf[tl:tl + CONV_HALO, :]

    for i in range(tl // rb):
        r0 = i * rb
        rows = pl.ds(r0, rb)
        xbuf[SSM_CONV_HALO + r0:SSM_CONV_HALO + r0 + rb, :] = _dot(hb[r0:r0 + rb, :], win_ref[:, c_xbc:c_dt])
        for c0 in range(0, xbuf.shape[1], LANES):
            cols = slice(c0, c0 + LANES)
            v = _silu(_causal_conv_block(xbuf, sw_ref, SSM_CONV_HALO, r0, rb, cols) + sb_ref[:, cols])
            if c0 < d_inner:
                xs_s[rows, cols] = v
            elif c0 < d_inner + gn:
                b_s[rows, c0 - d_inner:c0 - d_inner + LANES] = v
            else:
                c_s[rows, c0 - d_inner - gn:c0 - d_inner - gn + LANES] = v
    xbuf[0:SSM_CONV_HALO, :] = xbuf[tl:tl + SSM_CONV_HALO, :]

    dt = _softplus(_dot(hb, win_ref[:, c_dt:c_dt + LANES]) + dtb_ref[...])
    a_neg = -jnp.exp(alog_ref[...])
    acol_s[...] = dt * a_neg
    hh = lax.broadcasted_iota(jnp.int32, (LANES, d_inner), 0)
    cc = lax.broadcasted_iota(jnp.int32, (LANES, d_inner), 1)
    expand = (cc // SSM_HEAD_DIM == hh).astype(BF16)
    y_s[...] = xs_s[...] * _expand_lanes(dt, expand)

    causal = _tri_incl(q)
    tri = causal.astype(F32)
    lane = lax.broadcasted_iota(jnp.int32, (q, LANES), 1)
    left = lane < SSM_HEAD_DIM

    n_chunks = tl // q
    groups = [(slice(gi * SSM_STATE, (gi + 1) * SSM_STATE), slice(gi * hpg_cols, (gi + 1) * hpg_cols))
              for gi in range(SSM_GROUPS)]

    chunk_decay = []
    for c in range(n_chunks):
        rows = slice(c * q, (c + 1) * q)
        acs = _cumsum_rows(tri, acol_s[rows, :])
        acol_s[rows, :] = acs
        a_last = acs[q - 1:q, :]
        factors = jnp.concatenate([jnp.exp(a_last - acs), jnp.broadcast_to(jnp.exp(a_last), (SUBLANES, LANES))], 0)
        factors = _expand_lanes(factors, expand)
        chunk_decay.append(factors[q:q + 1])
        xend_b = (y_s[rows, :] * factors[0:q]).astype(BF16)
        for gi, (ncols, gcols) in enumerate(groups):
            sc_s[c, gi] = _dot(b_s[rows, ncols].T.astype(BF16), xend_b[:, gcols])

    for gi, (_, gcols) in enumerate(groups):
        st = st_ref[gi]
        for c in range(n_chunks):
            sb_s[c, gi] = st.astype(BF16)
            st = st * chunk_decay[c][:, gcols] + sc_s[c, gi]
        st_ref[gi] = st

    for c in range(n_chunks):
        rows = slice(c * q, (c + 1) * q)
        acs = acol_s[rows, :]
        acs_t = acs.T
        xdt_b = y_s[rows, :].astype(BF16)
        for gi, (ncols, gcols) in enumerate(groups):
            cg = c_s[rows, ncols]
            cb = _dot_nt(cg.astype(BF16), b_s[rows, ncols].astype(BF16))
            for pair in range(hpg_cols // LANES):
                col0 = gi * hpg_cols + pair * LANES
                xp = xdt_b[:, col0:col0 + LANES]
                sp = sb_s[c, gi, :, pair * LANES:(pair + 1) * LANES]
                yp = None
                for half in range(LANES // SSM_HEAD_DIM):
                    hd = col0 // SSM_HEAD_DIM + half
                    a_l = jnp.broadcast_to(acs[:, hd:hd + 1], (q, q))
                    m_h = cb * jnp.exp(jnp.where(causal, a_l - acs_t[hd:hd + 1, :], -jnp.inf))
                    lhs = jnp.concatenate([m_h.astype(BF16), (cg * jnp.exp(a_l)).astype(BF16)], 1)
                    keep = left if half == 0 else jnp.logical_not(left)
                    rhs = jnp.concatenate([jnp.where(keep, xp, jnp.zeros_like(xp)),
                                           jnp.where(keep, sp, jnp.zeros_like(sp))], 0)
                    part = _dot(lhs, rhs)
                    yp = part if yp is None else yp + part
                y_s[rows, col0:col0 + LANES] = yp

    z = _dot(hb, win_ref[:, c_z:c_xbc])
    y = (y_s[...] + xs_s[...] * dskip_ref[...]) * _silu(z)
    outs = []
    for gi in range(SSM_GROUPS):
        gcols = slice(gi * hpg_cols, (gi + 1) * hpg_cols)
        yg = y[:, gcols]
        yg = yg * lax.rsqrt(jnp.mean(yg * yg, -1, keepdims=True) + LN_EPS) * sng_ref[:, gcols]
        outs.append(yg.astype(BF16))
    m = _dot(u_s[...], wo_ref[0:conv_ch, :]) + _dot(jnp.concatenate(outs, -1), wo_ref[conv_ch:, :])
    o_ref[0] = _layer_norm(alpha * h + m, g_ref[...], b_ref[...])


def _even_mixer(h, w_in, conv_w, conv_b, conv_ln_g, conv_ln_b, ssm_conv_w, ssm_conv_b, dt_bias, a_log,
                d_skip, ssm_norm_g, w_out, g, b, alpha):
    bsz, l, d = h.shape
    conv_ch = conv_w.shape[1]
    xbc_w = ssm_conv_w.shape[1]
    n_heads = a_log.shape[0]
    d_inner = n_heads * SSM_HEAD_DIM
    gn = SSM_GROUPS * SSM_STATE
    assert xbc_w == d_inner + 2 * gn
    pad_h = LANES - n_heads
    assert w_in.shape[1] == 2 * conv_ch + d_inner + xbc_w + n_heads
    win = jnp.pad(w_in.astype(BF16), ((0, 0), (0, pad_h)))
    dtb = jnp.pad(dt_bias, (0, pad_h)).reshape(1, LANES)
    alog = jnp.pad(a_log, (0, pad_h)).reshape(1, LANES)
    dskip = jnp.repeat(d_skip, SSM_HEAD_DIM).reshape(1, d_inner)
    wo = w_out.astype(BF16)
    tl = EVEN_TILE
    row = pl.BlockSpec((1, tl, d), lambda bi, li: (bi, li, 0))
    cs = _const_spec
    state = (SSM_GROUPS, SSM_STATE, d_inner // SSM_GROUPS)
    scratch = [pltpu.VMEM((tl + CONV_HALO, conv_ch), F32),
               pltpu.VMEM((tl + SSM_CONV_HALO, xbc_w), F32),
               pltpu.VMEM((tl, d_inner), F32), pltpu.VMEM((tl, gn), F32), pltpu.VMEM((tl, gn), F32),
               pltpu.VMEM((tl, LANES), F32), pltpu.VMEM((tl, d_inner), F32),
               pltpu.VMEM((tl, conv_ch), BF16), pltpu.VMEM((tl, conv_ch), F32),
               pltpu.VMEM(state, F32),
               pltpu.VMEM((tl // SSM_CHUNK,) + state, F32), pltpu.VMEM((tl // SSM_CHUNK,) + state, BF16)]
    return pl.pallas_call(
        functools.partial(_even_kernel, alpha=alpha),
        grid=(bsz, l // tl),
        in_specs=[row, cs(win.shape),
                  cs(conv_w.shape), cs((1, conv_ch)), cs((1, conv_ch)), cs((1, conv_ch)),
                  cs(ssm_conv_w.shape), cs((1, xbc_w)), cs((1, LANES)), cs((1, LANES)), cs((1, d_inner)),
                  cs((1, d_inner)), cs(wo.shape), cs((1, d)), cs((1, d))],
        out_specs=row,
        out_shape=jax.ShapeDtypeStruct((bsz, l, d), F32),
        scratch_shapes=scratch,
        compiler_params=_params(2),
        name="conv_ssd_mixer",
    )(h, win, conv_w, conv_b.reshape(1, -1), conv_ln_g.reshape(1, -1),
      conv_ln_b.reshape(1, -1), ssm_conv_w, ssm_conv_b.reshape(1, -1), dtb, alog, dskip,
      ssm_norm_g.reshape(1, -1), wo, g, b)


def kernel(x, mem, even_w_in, even_conv_w, even_conv_b, even_conv_ln_g, even_conv_ln_b, even_ssm_conv_w,
           even_ssm_conv_b, even_dt_bias, even_a_log, even_d_skip, even_ssm_norm_g, even_w_out, odd_w_in,
           odd_w_gate2, odd_b_gate, odd_head_norm_g, odd_w_out, xa_w_q, xa_w_k, xa_w_v, xa_w_o, mlp_w1,
           mlp_w2, ln_g, ln_b):
    bsz, l, d = x.shape
    depth = ln_g.shape[0]
    alpha = float((2 * depth) ** 0.25)
    kt_all, v_all = _kv(mem, xa_w_k.astype(BF16), xa_w_v.astype(BF16))
    wq_all, wo_all = xa_w_q.astype(BF16), xa_w_o.astype(BF16)
    w1_all, w2_all = mlp_w1.astype(BF16), mlp_w2.astype(BF16)
    h = x
    for i in range(depth):
        j = i // 2
        g = [ln_g[i, s].reshape(1, d) for s in range(3)]
        b = [ln_b[i, s].reshape(1, d) for s in range(3)]
        if i % 2 == 0:
            h = _even_mixer(h, even_w_in[j], even_conv_w[j], even_conv_b[j], even_conv_ln_g[j],
                            even_conv_ln_b[j], even_ssm_conv_w[j], even_ssm_conv_b[j], even_dt_bias[j],
                            even_a_log[j], even_d_skip[j], even_ssm_norm_g[j], even_w_out[j], g[0], b[0], alpha)
        else:
            h = _gla_mixer(h, odd_w_in[j], odd_w_gate2[j], odd_b_gate[j], odd_head_norm_g[j], odd_w_out[j],
                           g[0], b[0], alpha)
        h = _xattn(h, kt_all, v_all, wq_all, wo_all, i, g[1], b[1], alpha)
        h = _mlp(h.reshape(bsz * l, d), w1_all, w2_all, i, g[2], b[2], alpha).reshape(bsz, l, d)
    return h
```
